```python
import math
import jax, jax.numpy as jnp
from jax import lax
import numpy as np

D_MODEL = 1024
BATCH = 16
SEQ = 2048
DEPTH = 2
DEC_BATCH = 32
DEC_SEQ = 8
PAST_LEN = 16384
PAGE_SIZE = 128

HG_HEADS = D_MODEL // 128
HG_DK = 128
HG_DV = 128
HG_CHUNK = 64
DF_HEADS = D_MODEL // 128
DF_DH = 64
DF_E = 2 * DF_DH
Q_BLOCK = 128
SSM_D_INNER = 2 * D_MODEL
SSM_HEADDIM = 64
SSM_HEADS = SSM_D_INNER // SSM_HEADDIM
SSM_GROUPS = 8
SSM_HPG = SSM_HEADS // SSM_GROUPS
SSM_STATE = 128
SSM_CONV = 4
SSM_CONV_DIM = SSM_D_INNER + 2 * SSM_GROUPS * SSM_STATE
SSM_CHUNK = 128
MEM_LEN = 256
MEM_HEADS = 4
MEM_DH = D_MODEL // MEM_HEADS
D_FF = 2816
FFN_CONV = 3
N_BRANCH = 3
EPS = 1e-6
MASK_VALUE = -1e30
IN_WIDTHS = (HG_HEADS * HG_DK, HG_HEADS * HG_DK, HG_HEADS * HG_DV, HG_HEADS * HG_DV,
             DF_HEADS * 2 * DF_DH, DF_HEADS * 2 * DF_DH, DF_HEADS * DF_E,
             SSM_D_INNER, SSM_CONV_DIM, SSM_HEADS, N_BRANCH * D_MODEL)
IN_COLS = sum(IN_WIDTHS)
SPLIT_POINTS = tuple(int(v) for v in np.cumsum(IN_WIDTHS)[:-1])

kernel_name = 'hybrid_hgrn2_diffattn_ssd_gated_merge_step'


def rms_norm(x, w):
    xf = x.astype(jnp.float32)
    y = xf * lax.rsqrt(jnp.mean(xf * xf, axis=-1, keepdims=True) + EPS)
    return (y * w.astype(jnp.float32)).astype(x.dtype)


def causal_conv(x, buf, w, b):
    t = x.shape[1]
    xp = jnp.concatenate([buf.astype(x.dtype), x], axis=1)
    out = b + w[0] * xp[:, :t]
    for j in range(1, w.shape[0]):
        out = out + w[j] * xp[:, j:j + t]
    return out, xp[:, t:]


def alibi_slopes(n):
    return jnp.exp2(-8.0 * jnp.arange(1, n + 1, dtype=jnp.float32) / n)


def to_chunks(v, n, c):
    return v.reshape(v.shape[0], n, c, *v.shape[2:]).swapaxes(0, 1)


def masked_decay(mask, diff):
    return jnp.where(mask, jnp.exp(jnp.where(mask, diff, 0.0)), 0.0)


def gla_chunked(q, k, v, log_f, s0):
    bsz, t = q.shape[:2]
    c = math.gcd(HG_CHUNK, t)
    n = t // c
    mask = jnp.tril(jnp.ones((c, c), dtype=bool))[None, :, :, None, None]

    def step(s, inp):
        qc, kc, vc, gc = inp
        b = jnp.cumsum(gc, axis=1)
        inter = jnp.einsum('bthk,bhkv->bthv', qc * jnp.exp(b), s)
        decay = masked_decay(mask, b[:, :, None] - b[:, None, :])
        scores = jnp.einsum('bthk,btshk,bshk->bhts', qc, decay, kc)
        intra = jnp.einsum('bhts,bshv->bthv', scores, vc)
        last = b[:, -1]
        s = jnp.exp(last)[..., None] * s + jnp.einsum('bshk,bshv->bhkv', kc * jnp.exp(last[:, None] - b), vc)
        return s, inter + intra

    s, o = lax.scan(step, s0.astype(jnp.float32),
                    (to_chunks(q, n, c), to_chunks(k, n, c), to_chunks(v, n, c), to_chunks(log_f, n, c)))
    return o.swapaxes(0, 1).reshape(bsz, t, *o.shape[3:]), s


def ssd_chunked(x, dt, a, b, c_in, s0):
    bsz, t = x.shape[:2]
    c = math.gcd(SSM_CHUNK, t)
    n = t // c
    mask = jnp.tril(jnp.ones((c, c), dtype=bool))[None, :, :, None, None]

    def step(s, inp):
        xc, dtc, bc, cc = inp
        cum = jnp.cumsum(dtc * a, axis=1)
        seg = masked_decay(mask, cum[:, :, None] - cum[:, None, :])
        cb = jnp.einsum('btgn,bsgn->btsg', cc, bc)
        y = jnp.einsum('btsg,btsgr,bsgr,bsgrp->btgrp', cb, seg, dtc, xc)
        y = y + jnp.einsum('btgn,bgrpn->btgrp', cc, s) * jnp.exp(cum)[..., None]
        last = cum[:, -1]
        s = jnp.exp(last)[..., None, None] * s + jnp.einsum(
            'bsgn,bsgr,bsgrp->bgrpn', bc, dtc * jnp.exp(last[:, None] - cum), xc)
        return s, y

    s, y = lax.scan(step, s0.astype(jnp.float32),
                    (to_chunks(x, n, c), to_chunks(dt, n, c), to_chunks(b, n, c), to_chunks(c_in, n, c)))
    return y.swapaxes(0, 1).reshape(x.shape), s


def diff_attend(q, k, v, dist, slopes, lam):
    s = jnp.einsum('bqhcd,bkhcd->bhcqk', q, k).astype(jnp.float32) * (DF_DH ** -0.5)
    bias = jnp.where(dist >= 0, -slopes[:, None, None] * dist.astype(jnp.float32), MASK_VALUE)
    p = jax.nn.softmax(s + bias[None, :, None], axis=-1)
    a = p[:, :, 0] - lam * p[:, :, 1]
    return jnp.einsum('bhqk,bkhe->bqhe', a.astype(v.dtype), v)


def attend_prompt(q, k, v, slopes, lam):
    bsz, t = q.shape[:2]
    qb = math.gcd(Q_BLOCK, t)
    n = t // qb
    q_blocks = q.reshape(bsz, n, qb, *q.shape[2:]).swapaxes(0, 1)
    k_pos = jnp.arange(t)

    def one(args):
        qi, i = args
        q_pos = i * qb + jnp.arange(qb)
        return diff_attend(qi, k, v, q_pos[:, None] - k_pos[None, :], slopes, lam)

    o = lax.map(one, (q_blocks, jnp.arange(n)))
    return o.swapaxes(0, 1).reshape(bsz, t, DF_HEADS, DF_E)


def attend_sample(q, k, v, cache_k, cache_v, page_table, layer, slopes, lam):
    ts = q.shape[1]
    past = page_table.shape[1] * cache_k.shape[2]
    k_pos = jnp.arange(past + ts)
    q_pos = past + jnp.arange(ts)
    dist = q_pos[:, None] - k_pos[None, :]

    def one(args):
        qi, ki, vi, pages = args
        kp = cache_k[layer, pages].reshape(past, DF_HEADS, 2, DF_DH)
        vp = cache_v[layer, pages].reshape(past, DF_HEADS, DF_E)
        kk = jnp.concatenate([kp.astype(ki.dtype), ki], axis=0)
        vv = jnp.concatenate([vp.astype(vi.dtype), vi], axis=0)
        return diff_attend(qi[None], kk[None], vv[None], dist, slopes, lam)[0]

    return lax.map(one, (q, k, v, page_table))


def trunk_layer(x, mem_k, mem_v, hgrn_s0, ssm_s0, conv_buf, ffn_buf, p, lower_bound, lam_init, attend):
    bsz, t, _ = x.shape
    dty = x.dtype
    h = rms_norm(x, p['norm_mix'])
    (q_a, f_a, i_a, g_a, q_b, k_b, v_b, z_c, xbc_c, dt_c, gates) = jnp.split(h @ p['w_in'], SPLIT_POINTS, axis=-1)

    forget = lower_bound + (1.0 - lower_bound) * jax.nn.sigmoid(f_a.astype(jnp.float32))
    log_f = jnp.log(forget)
    k_a = 1.0 - forget
    shp = (bsz, t, HG_HEADS, HG_DK)
    o_a, s_a = gla_chunked(jax.nn.silu(q_a).reshape(shp), k_a.reshape(shp),
                           i_a.reshape(bsz, t, HG_HEADS, HG_DV), log_f.reshape(shp), hgrn_s0)
    o_a = rms_norm(o_a.astype(dty), p['hgrn_out_norm']) * jax.nn.silu(g_a.reshape(bsz, t, HG_HEADS, HG_DV))
    o_a = o_a.reshape(bsz, t, -1)

    lam_q1, lam_k1, lam_q2, lam_k2 = p['diff_lambda'].astype(jnp.float32)
    lam = jnp.exp(jnp.sum(lam_q1 * lam_k1)) - jnp.exp(jnp.sum(lam_q2 * lam_k2)) + lam_init
    kb = k_b.reshape(bsz, t, DF_HEADS, 2, DF_DH)
    vb = v_b.reshape(bsz, t, DF_HEADS, DF_E)
    o_b = attend(q_b.reshape(bsz, t, DF_HEADS, 2, DF_DH), kb, vb, lam)
    o_b = (rms_norm(o_b, p['diff_subln']) * (1.0 - lam_init)).reshape(bsz, t, -1)

    xbc, conv_new = causal_conv(xbc_c, conv_buf, p['ssm_conv_w'], p['ssm_conv_b'])
    xbc = jax.nn.silu(xbc)
    x_c, b_c, c_c = jnp.split(xbc, (SSM_D_INNER, SSM_D_INNER + SSM_GROUPS * SSM_STATE), axis=-1)
    dt = jax.nn.softplus(dt_c.astype(jnp.float32) + p['ssm_dt_bias'].astype(jnp.float32))
    a = -jnp.exp(p['ssm_a_log'].astype(jnp.float32))
    xg = x_c.reshape(bsz, t, SSM_GROUPS, SSM_HPG, SSM_HEADDIM)
    y_c, s_c = ssd_chunked(xg, dt.reshape(bsz, t, SSM_GROUPS, SSM_HPG), a.reshape(SSM_GROUPS, SSM_HPG),
                           b_c.reshape(bsz, t, SSM_GROUPS, SSM_STATE), c_c.reshape(bsz, t, SSM_GROUPS, SSM_STATE), ssm_s0)
    y_c = y_c + p['ssm_d'].reshape(SSM_GROUPS, SSM_HPG)[:, :, None] * xg
    y_c = y_c.reshape(bsz, t, SSM_D_INNER).astype(dty) * jax.nn.silu(z_c)
    y_c = rms_norm(y_c.reshape(bsz, t, SSM_GROUPS, -1), p['ssm_norm'].reshape(SSM_GROUPS, -1)).reshape(bsz, t, -1)

    gate = jax.nn.sigmoid(gates).reshape(bsz, t, N_BRANCH, D_MODEL)
    merged = (gate[:, :, 0] * (o_a @ p['w_br_a']) + gate[:, :, 1] * (o_b @ p['w_br_b'])
              + gate[:, :, 2] * (y_c @ p['w_br_c']))
    x = x + merged @ p['w_out']

    hm = rms_norm(x, p['norm_mem_q'])
    q_m = (hm @ p['w_mq']).reshape(bsz, t, MEM_HEADS, MEM_DH)
    sm = jnp.einsum('bthd,bmhd->bhtm', q_m, mem_k).astype(jnp.float32) * (MEM_DH ** -0.5)
    pm = jax.nn.softmax(sm, axis=-1)
    o_m = jnp.einsum('bhtm,bmhd->bthd', pm.astype(mem_v.dtype), mem_v).reshape(bsz, t, D_MODEL)
    x = x + o_m @ p['w_mo']

    hf = rms_norm(x, p['norm_ffn'])
    g_f, v_f = jnp.split(hf @ p['w_up'], 2, axis=-1)
    g_conv, ffn_new = causal_conv(g_f, ffn_buf, p['ffn_conv_w'], p['ffn_conv_b'])
    x = x + (jax.nn.silu(g_conv) * v_f) @ p['w_down']

    return x, (kb.reshape(bsz, t, DF_HEADS, DF_E), vb, s_a, s_c, conv_new, ffn_new)


def setup_inputs(seed: int = 0) -> dict:
    key = jax.random.key(seed)
    keys = jax.random.split(key, 40)
    n_pages = PAST_LEN // PAGE_SIZE
    n_used = DEC_BATCH * n_pages
    n_phys = n_used + n_used // 4

    def nrm(i, shape, scale=1.0):
        return scale * jax.random.normal(keys[i], shape, dtype=jnp.float32)

    def gain(i, shape):
        return 1.0 + nrm(i, shape, 0.02)

    page_table = jax.random.permutation(keys[10], n_phys)[:n_used].reshape(DEC_BATCH, n_pages).astype(jnp.int32)
    dt_init = jnp.exp(jax.random.uniform(keys[20], (DEPTH, SSM_HEADS), dtype=jnp.float32,
                                         minval=math.log(1e-3), maxval=math.log(1e-1)))
    ssm_dt_bias = dt_init + jnp.log(-jnp.expm1(-dt_init))
    ssm_a_log = jnp.log(jax.random.uniform(keys[21], (DEPTH, SSM_HEADS), dtype=jnp.float32, minval=1.0, maxval=16.0))
    return {
        'x_prompt': nrm(0, (BATCH, SEQ, D_MODEL)),
        'x_sample': nrm(1, (DEC_BATCH, DEC_SEQ, D_MODEL)),
        'cache_k': nrm(2, (DEPTH, n_phys, PAGE_SIZE, DF_HEADS, DF_E)),
        'cache_v': nrm(3, (DEPTH, n_phys, PAGE_SIZE, DF_HEADS, DF_E)),
        'cache_mem_k': nrm(4, (DEPTH, DEC_BATCH, MEM_LEN, MEM_HEADS, MEM_DH)),
        'cache_mem_v': nrm(5, (DEPTH, DEC_BATCH, MEM_LEN, MEM_HEADS, MEM_DH)),
        'state_hgrn': nrm(6, (DEPTH, DEC_BATCH, HG_HEADS, HG_DK, HG_DV), 0.5),
        'state_ssm': nrm(7, (DEPTH, DEC_BATCH, SSM_GROUPS, SSM_HPG, SSM_HEADDIM, SSM_STATE), 0.1),
        'state_conv': nrm(8, (DEPTH, DEC_BATCH, SSM_CONV - 1, SSM_CONV_DIM)),
        'state_ffn_conv': nrm(9, (DEPTH, DEC_BATCH, FFN_CONV - 1, D_FF)),
        'page_table': page_table,
        'mem_prompt': nrm(11, (BATCH, MEM_LEN, D_MODEL)),
        'norm_mix': gain(12, (DEPTH, D_MODEL)),
        'w_in': nrm(13, (DEPTH, D_MODEL, IN_COLS), D_MODEL ** -0.5),
        'hgrn_lb_logits': nrm(14, (DEPTH, HG_HEADS * HG_DK), 0.5),
        'hgrn_out_norm': gain(15, (DEPTH, HG_DV)),
        'diff_lambda': nrm(16, (DEPTH, 4, DF_DH), 0.1),
        'diff_subln': gain(17, (DEPTH, DF_E)),
        'ssm_conv_w': nrm(18, (DEPTH, SSM_CONV, SSM_CONV_DIM), SSM_CONV ** -0.5),
        'ssm_conv_b': nrm(19, (DEPTH, SSM_CONV_DIM), 0.01),
        'ssm_dt_bias': ssm_dt_bias,
        'ssm_a_log': ssm_a_log,
        'ssm_d': 1.0 + nrm(22, (DEPTH, SSM_HEADS), 0.1),
        'ssm_norm': gain(23, (DEPTH, SSM_D_INNER)),
        'w_br_a': nrm(24, (DEPTH, HG_HEADS * HG_DV, D_MODEL), (HG_HEADS * HG_DV) ** -0.5),
        'w_br_b': nrm(25, (DEPTH, DF_HEADS * DF_E, D_MODEL), (DF_HEADS * DF_E) ** -0.5),
        'w_br_c': nrm(26, (DEPTH, SSM_D_INNER, D_MODEL), SSM_D_INNER ** -0.5),
        'w_out': nrm(27, (DEPTH, D_MODEL, D_MODEL), D_MODEL ** -0.5),
        'norm_mem_q': gain(28, (DEPTH, D_MODEL)),
        'norm_mem_kv': gain(29, (DEPTH, D_MODEL)),
        'w_mq': nrm(30, (DEPTH, D_MODEL, D_MODEL), D_MODEL ** -0.5),
        'w_mk': nrm(31, (DEPTH, D_MODEL, D_MODEL), D_MODEL ** -0.5),
        'w_mv': nrm(32, (DEPTH, D_MODEL, D_MODEL), D_MODEL ** -0.5),
        'w_mo': nrm(33, (DEPTH, D_MODEL, D_MODEL), D_MODEL ** -0.5),
        'norm_ffn': gain(34, (DEPTH, D_MODEL)),
        'w_up': nrm(35, (DEPTH, D_MODEL, 2 * D_FF), D_MODEL ** -0.5),
        'ffn_conv_w': nrm(36, (DEPTH, FFN_CONV, D_FF), FFN_CONV ** -0.5),
        'ffn_conv_b': nrm(37, (DEPTH, D_FF), 0.01),
        'w_down': nrm(38, (DEPTH, D_FF, D_MODEL), D_FF ** -0.5),
        'norm_final': gain(39, (D_MODEL,)),
    }


def reference(x_prompt, x_sample, cache_k, cache_v, cache_mem_k, cache_mem_v, state_hgrn, state_ssm,
              state_conv, state_ffn_conv, page_table, mem_prompt,
              norm_mix, w_in, hgrn_lb_logits, hgrn_out_norm, diff_lambda, diff_subln,
              ssm_conv_w, ssm_conv_b, ssm_dt_bias, ssm_a_log, ssm_d, ssm_norm,
              w_br_a, w_br_b, w_br_c, w_out, norm_mem_q, norm_mem_kv, w_mq, w_mk, w_mv, w_mo,
              norm_ffn, w_up, ffn_conv_w, ffn_conv_b, w_down, norm_final):
    lb_soft = jax.nn.softmax(hgrn_lb_logits.astype(jnp.float32), axis=0)
    lower_bounds = jnp.cumsum(lb_soft, axis=0) - lb_soft[0]
    slopes = alibi_slopes(DF_HEADS)
    bp, tp = x_prompt.shape[:2]
    xp, xs = x_prompt, x_sample
    kp_l, vp_l, ks_l, vs_l, mk_l, mv_l = [], [], [], [], [], []
    hp_l, hs_l, sp_l, ss_l, cp_l, cs_l, fp_l, fs_l = [], [], [], [], [], [], [], []
    for l in range(DEPTH):
        p = {'norm_mix': norm_mix[l], 'w_in': w_in[l], 'hgrn_out_norm': hgrn_out_norm[l],
             'diff_lambda': diff_lambda[l], 'diff_subln': diff_subln[l],
             'ssm_conv_w': ssm_conv_w[l], 'ssm_conv_b': ssm_conv_b[l], 'ssm_dt_bias': ssm_dt_bias[l],
             'ssm_a_log': ssm_a_log[l], 'ssm_d': ssm_d[l], 'ssm_norm': ssm_norm[l],
             'w_br_a': w_br_a[l], 'w_br_b': w_br_b[l], 'w_br_c': w_br_c[l], 'w_out': w_out[l],
             'norm_mem_q': norm_mem_q[l], 'w_mq': w_mq[l], 'w_mo': w_mo[l],
             'norm_ffn': norm_ffn[l], 'w_up': w_up[l], 'ffn_conv_w': ffn_conv_w[l],
             'ffn_conv_b': ffn_conv_b[l], 'w_down': w_down[l]}
        lam_init = 0.8 - 0.6 * math.exp(-0.3 * l)
        mem_n = rms_norm(mem_prompt, norm_mem_kv[l])
        mk = (mem_n @ w_mk[l]).reshape(bp, -1, MEM_HEADS, MEM_DH)
        mv = (mem_n @ w_mv[l]).reshape(bp, -1, MEM_HEADS, MEM_DH)
        xp, st_p = trunk_layer(
            xp, mk, mv,
            jnp.zeros((bp, HG_HEADS, HG_DK, HG_DV), jnp.float32),
            jnp.zeros((bp, SSM_GROUPS, SSM_HPG, SSM_HEADDIM, SSM_STATE), jnp.float32),
            jnp.zeros((bp, SSM_CONV - 1, SSM_CONV_DIM), xp.dtype),
            jnp.zeros((bp, FFN_CONV - 1, D_FF), xp.dtype),
            p, lower_bounds[l], lam_init,
            lambda q, k, v, lam: attend_prompt(q, k, v, slopes, lam))
        xs, st_s = trunk_layer(
            xs, cache_mem_k[l], cache_mem_v[l], state_hgrn[l], state_ssm[l], state_conv[l], state_ffn_conv[l],
            p, lower_bounds[l], lam_init,
            lambda q, k, v, lam, layer=l: attend_sample(q, k, v, cache_k, cache_v, page_table, layer, slopes, lam))
        kp_l.append(st_p[0]); vp_l.append(st_p[1]); ks_l.append(st_s[0]); vs_l.append(st_s[1])
        mk_l.append(mk); mv_l.append(mv)
        hp_l.append(st_p[2]); hs_l.append(st_s[2]); sp_l.append(st_p[3]); ss_l.append(st_s[3])
        cp_l.append(st_p[4]); cs_l.append(st_s[4]); fp_l.append(st_p[5]); fs_l.append(st_s[5])
    y_prompt = rms_norm(xp, norm_final)
    y_sample = rms_norm(xs, norm_final)
    return (y_prompt, y_sample,
            jnp.stack(kp_l), jnp.stack(vp_l), jnp.stack(ks_l), jnp.stack(vs_l),
            jnp.stack(mk_l), jnp.stack(mv_l),
            jnp.stack(hp_l), jnp.stack(hs_l),
            jnp.stack(sp_l), jnp.stack(ss_l),
            jnp.stack(cp_l), jnp.stack(cs_l),
            jnp.stack(fp_l), jnp.stack(fs_l))
```

```python
import functools
import math

import jax
import jax.numpy as jnp
from jax import lax
from jax.experimental import pallas as pl
from jax.experimental.pallas import tpu as pltpu

F32 = jnp.float32
BF16 = jnp.bfloat16
EPS = 1e-6
MASK_VALUE = -1e30
LANES = 128
SUBLANES = 8
HGRN_CHUNK = 64
SSD_CHUNK = 128
VMEM_LIMIT = 48 * 1024 * 1024

_TRANS_B = (((1,), (1,)), ((), ()))
_TRANS_A = (((0,), (0,)), ((), ()))


def _params(*sem):
    return pltpu.CompilerParams(dimension_semantics=sem, vmem_limit_bytes=VMEM_LIMIT)


def _silu(x):
    return x * jax.nn.sigmoid(x)


def _rms(x, w):
    return x * lax.rsqrt(jnp.mean(x * x, axis=-1, keepdims=True) + EPS) * w


def _pad_rows(x, rows):
    if x.shape[0] == rows:
        return x
    return jnp.concatenate([x, jnp.zeros((rows - x.shape[0],) + x.shape[1:], x.dtype)], axis=0)


def _tril(n):
    r = lax.broadcasted_iota(jnp.int32, (n, n), 0)
    c = lax.broadcasted_iota(jnp.int32, (n, n), 1)
    return r >= c


def _select_sum_rows(w01, x):
    n = x.shape[1]
    hi = x.astype(BF16)
    r1 = x - hi.astype(F32)
    mid = r1.astype(BF16)
    lo = (r1 - mid.astype(F32)).astype(BF16)
    y = jnp.dot(w01.astype(BF16), jnp.concatenate([hi, mid, lo], axis=1), preferred_element_type=F32)
    return y[:, :n] + y[:, n:2 * n] + y[:, 2 * n:]


def _cumsum_rows(x):
    return _select_sum_rows(_tril(x.shape[0]).astype(F32), x)


def _pick_tile(n, candidates):
    for c in candidates:
        if n % c == 0:
            return c
    return n


def _norm_matmul_kernel(x_ref, g_ref, w_ref, o_ref, xn_ref):
    @pl.when(pl.program_id(1) == 0)
    def _():
        xn_ref[...] = _rms(x_ref[...], g_ref[...]).astype(BF16)

    o_ref[...] = jnp.dot(xn_ref[...], w_ref[...], preferred_element_type=F32)


def _norm_matmul(x, g, w):
    n, d = x.shape
    m = w.shape[1]
    tm = _pick_tile(n, (1024, 512, 256, 128, 64, 32, 16, 8))
    tn = _pick_tile(m, (1024, 512, 256, 128))
    return pl.pallas_call(
        _norm_matmul_kernel,
        out_shape=jax.ShapeDtypeStruct((n, m), F32),
        grid=(n // tm, m // tn),
        in_specs=[pl.BlockSpec((tm, d), lambda i, j: (i, 0)),
                  pl.BlockSpec((1, d), lambda i, j: (0, 0)),
                  pl.BlockSpec((d, tn), lambda i, j: (0, j))],
        out_specs=pl.BlockSpec((tm, tn), lambda i, j: (i, j)),
        scratch_shapes=[pltpu.VMEM((tm, d), BF16)],
        compiler_params=_params("parallel", "arbitrary"),
        name="norm_matmul",
    )(x, g.reshape(1, d), w)


def _hgrn_kernel(*refs, layer, has_s0):
    if has_s0:
        q_ref, f_ref, i_ref, g_ref, lb_ref, nw_ref, s0_ref, o_ref, so_ref, st_ref = refs
    else:
        q_ref, f_ref, i_ref, g_ref, lb_ref, nw_ref, o_ref, so_ref, st_ref = refs
    t = pl.program_id(2)
    tb = q_ref.shape[1]
    c = HGRN_CHUNK
    n_chunks = max(1, tb // c)
    rows = min(tb, c)

    @pl.when(t == 0)
    def _():
        if has_s0:
            st_ref[...] = s0_ref[0, 0].T
        else:
            st_ref[...] = jnp.zeros_like(st_ref)

    lbl = lb_ref[...]
    e = jnp.exp(lbl - jnp.max(lbl, axis=0, keepdims=True))
    soft = e / jnp.sum(e, axis=0, keepdims=True)
    lb = jnp.sum(soft[:layer + 1], axis=0, keepdims=True) - soft[0:1]

    nw = nw_ref[...]
    row_id = lax.broadcasted_iota(jnp.int32, (c, LANES), 0)
    r_i = lax.broadcasted_iota(jnp.int32, (c, c), 0)
    s_i = lax.broadcasted_iota(jnp.int32, (c, c), 1)
    assert 2 * c == LANES
    levels = [1 << j for j in range(int(math.log2(c)))]
    mids = [jnp.bitwise_and(r_i, -2 * m) + (m - 1) for m in levels]
    cum_w = jnp.concatenate([(s_i <= r_i).astype(F32)] + [(s_i <= mid).astype(F32) for mid in mids], axis=0)
    upper = [jnp.bitwise_and(row_id, m) != 0 for m in levels]
    group = 4
    slots = [None] + levels
    slots += [0] * (-len(slots) % group)
    lane_id = lax.broadcasted_iota(jnp.int32, (c, LANES), 1)
    key = jnp.bitwise_and(lane_id, c - 1)
    lane_half = lax.shift_right_logical(lane_id, int(math.log2(c)))
    slot_mask = []
    for si, m in enumerate(slots):
        in_half = lane_half == (si % 2)
        if m is None:
            slot_mask.append(jnp.logical_and(in_half, row_id == key))
        elif m > 0:
            slot_mask.append(jnp.logical_and(in_half, jnp.bitwise_and(row_id, -2 * m) == jnp.bitwise_and(key, -2 * m)))
        else:
            slot_mask.append(None)

    chunks = []
    for ci in range(n_chunks):
        sl = slice(ci * c, ci * c + rows)
        f = _pad_rows(f_ref[0, sl, :], c)
        q = _silu(_pad_rows(q_ref[0, sl, :], c))
        v = _pad_rows(i_ref[0, sl, :], c)
        forget = lb + (1.0 - lb) * jax.nn.sigmoid(f)
        logf = jnp.log(forget)
        kk = 1.0 - forget
        if rows < c:
            logf = jnp.where(row_id < rows, logf, 0.0)
            kk = jnp.where(row_id < rows, kk, 0.0)
        chunks.append((sl, q, v, kk, logf))
    logf_all = jnp.concatenate([ch[4] for ch in chunks], axis=1) if n_chunks > 1 else chunks[0][4]
    cums_all = _select_sum_rows(cum_w, logf_all)

    st = st_ref[...]
    zero_slot = jnp.zeros((c, LANES), BF16)
    for ci, (sl, q, v, kk, _) in enumerate(chunks):
        cums = cums_all[:, ci * LANES:(ci + 1) * LANES]
        b = cums[:c]
        v_b = v.astype(BF16)
        q_parts, k_parts = [q.astype(BF16)], [kk.astype(BF16)]
        for li in range(len(levels)):
            bref = cums[(li + 1) * c:(li + 2) * c]
            q_parts.append(jnp.where(upper[li], q * jnp.exp(b - bref), 0.0).astype(BF16))
            k_parts.append(jnp.where(upper[li], 0.0, kk * jnp.exp(bref - b)).astype(BF16))
        q_parts += [zero_slot] * (len(slots) - len(q_parts))
        k_parts += [zero_slot] * (len(slots) - len(k_parts))
        scores = jnp.zeros((c, LANES), F32)
        for g0 in range(0, len(slots), group):
            s_g = lax.dot_general(jnp.concatenate(q_parts[g0:g0 + group], axis=0),
                                  jnp.concatenate(k_parts[g0:g0 + group], axis=0), _TRANS_B,
                                  preferred_element_type=F32)
            for i in range(group):
                if slot_mask[g0 + i] is not None:
                    tile = s_g[i * c:(i + 1) * c, (i // 2) * LANES:(i // 2 + 1) * LANES]
                    scores = scores + jnp.where(slot_mask[g0 + i], tile, 0.0)
        o = lax.dot_general((q * jnp.exp(b)).astype(BF16), st.astype(BF16), _TRANS_B, preferred_element_type=F32)
        o = o + jnp.dot(scores.astype(BF16), jnp.concatenate([v_b, v_b], axis=0), preferred_element_type=F32)
        last = b[c - 1:c]
        kdec = (kk * jnp.exp(last - b)).astype(BF16)
        st = jnp.exp(last) * st + lax.dot_general(v.astype(BF16), kdec, _TRANS_A, preferred_element_type=F32)
        g = g_ref[0, sl, :]
        o_ref[0, sl, :] = _rms(o[:rows], nw) * _silu(g)
    st_ref[...] = st

    @pl.when(t == pl.num_programs(2) - 1)
    def _():
        so_ref[0, 0] = st.T


def _hgrn(proj, lb_logits, norm_w, s0, *, layer, n_heads, col0):
    bsz, t, _ = proj.shape
    tb = _pick_tile(t, (512, 256, 128, 64)) if t >= HGRN_CHUNK else t
    depth = lb_logits.shape[0]

    def col(k):
        return pl.BlockSpec((1, tb, LANES), lambda b, h, ti, k=k: (b, ti, col0 + k * n_heads + h))

    in_specs = [col(0), col(1), col(2), col(3),
                pl.BlockSpec((depth, LANES), lambda b, h, ti: (0, h)),
                pl.BlockSpec((1, LANES), lambda b, h, ti: (0, 0))]
    args = [proj, proj, proj, proj, lb_logits, norm_w.reshape(1, LANES)]
    if s0 is not None:
        in_specs.append(pl.BlockSpec((1, 1, LANES, LANES), lambda b, h, ti: (b, h, 0, 0)))
        args.append(s0)
    return pl.pallas_call(
        functools.partial(_hgrn_kernel, layer=layer, has_s0=s0 is not None),
        out_shape=(jax.ShapeDtypeStruct((bsz, t, n_heads * LANES), F32),
                   jax.ShapeDtypeStruct((bsz, n_heads, LANES, LANES), F32)),
        grid=(bsz, n_heads, t // tb),
        in_specs=in_specs,
        out_specs=(pl.BlockSpec((1, tb, LANES), lambda b, h, ti: (b, ti, h)),
                   pl.BlockSpec((1, 1, LANES, LANES), lambda b, h, ti: (b, h, 0, 0))),
        scratch_shapes=[pltpu.VMEM((LANES, LANES), F32)],
        compiler_params=_params("parallel", "parallel", "arbitrary"),
        name="hgrn",
    )(*args)


def _diff_lambda(lam_ref, lam_init):
    l4 = lam_ref[...]
    return (jnp.exp(jnp.sum(l4[0:1] * l4[1:2], axis=-1, keepdims=True))
            - jnp.exp(jnp.sum(l4[2:3] * l4[3:4], axis=-1, keepdims=True)) + lam_init)


def _softmax_update(s, v_b, m_ref, l_ref, acc_ref):
    m_old = m_ref[...]
    m_new = jnp.maximum(m_old, jnp.max(s, axis=-1, keepdims=True))
    alpha = jnp.exp(m_old - m_new)
    p = jnp.exp(s - m_new)
    l_ref[...] = alpha * l_ref[...] + jnp.sum(p, axis=-1, keepdims=True)
    acc_ref[...] = alpha * acc_ref[...] + jnp.dot(p.astype(BF16), v_b, preferred_element_type=F32)
    m_ref[...] = m_new


def _attn_prompt_kernel(slope_ref, q_ref, k_ref, v_ref, lam_ref, nw_ref, o_ref, kb_ref, vb_ref, bias_ref, bdiag_ref,
                        s_ref, m_ref, l_ref, acc_ref, *, blk, lam_init):
    h = pl.program_id(1)
    qi = pl.program_id(2)
    half = LANES // 2
    n_sub = blk // LANES
    neg_slope = -slope_ref[h]

    @pl.when(qi == 0)
    def _():
        kb_ref[...] = k_ref[0].astype(BF16)
        vb_ref[...] = v_ref[0].astype(BF16)
        r = lax.broadcasted_iota(jnp.int32, (2 * blk, blk), 0)
        r = jnp.where(r >= blk, r - blk, r)
        c = lax.broadcasted_iota(jnp.int32, (2 * blk, blk), 1)
        bias0 = neg_slope * (r - c).astype(F32)
        bias_ref[...] = bias0
        bdiag_ref[...] = jnp.where(c <= r, bias0, MASK_VALUE)

    q = q_ref[0] * (half ** -0.5)
    lane = lax.broadcasted_iota(jnp.int32, q.shape, 1)
    qs = jnp.concatenate([jnp.where(lane < half, q, 0.0), jnp.where(lane >= half, q, 0.0)], axis=0).astype(BF16)

    def scores(j, bias):
        start = pl.multiple_of(j * blk, blk)
        s = lax.dot_general(qs, kb_ref[pl.ds(start, blk), :], _TRANS_B, preferred_element_type=F32) + bias
        s_ref[j] = s
        m = m_ref[...]
        for u in range(n_sub):
            m = jnp.maximum(m, s[:, u * LANES:(u + 1) * LANES])
        m_ref[...] = m

    m_ref[...] = jnp.full_like(m_ref, MASK_VALUE)

    def pass1(j, carry):
        scores(j, bias_ref[...] + neg_slope * ((qi - j) * blk).astype(F32))
        return carry

    lax.fori_loop(0, qi, pass1, 0)
    scores(qi, bdiag_ref[...])
    m_row = jnp.broadcast_to(jnp.max(m_ref[...], axis=-1, keepdims=True), m_ref.shape)

    l_ref[...] = jnp.zeros_like(l_ref)
    acc_ref[...] = jnp.zeros_like(acc_ref)

    def pass2(j, carry):
        start = pl.multiple_of(j * blk, blk)
        s = s_ref[j]
        l = l_ref[...]
        ps = []
        for u in range(n_sub):
            p = jnp.exp(s[:, u * LANES:(u + 1) * LANES] - m_row)
            l = l + p
            ps.append(p.astype(BF16))
        l_ref[...] = l
        acc_ref[...] += jnp.dot(jnp.concatenate(ps, axis=1), vb_ref[pl.ds(start, blk), :], preferred_element_type=F32)
        return carry

    lax.fori_loop(0, qi + 1, pass2, 0)

    lam = _diff_lambda(lam_ref, lam_init)
    acc = acc_ref[...]
    l = jnp.sum(l_ref[...], axis=-1, keepdims=True)
    o = acc[:blk] / l[:blk] - lam * (acc[blk:] / l[blk:])
    o_ref[0] = _rms(o, nw_ref[...]) * (1.0 - lam_init)


def _attn_prompt(proj, slopes, lam_p, subln_w, *, n_heads, col0, lam_init):
    bsz, t, _ = proj.shape
    blk = _pick_tile(t, (512, 256, 128))
    return pl.pallas_call(
        functools.partial(_attn_prompt_kernel, blk=blk, lam_init=lam_init),
        out_shape=jax.ShapeDtypeStruct((bsz, t, n_heads * LANES), F32),
        grid=(bsz, n_heads, t // blk),
        in_specs=[pl.BlockSpec(memory_space=pltpu.SMEM),
                  pl.BlockSpec((1, blk, LANES), lambda b, h, qi: (b, qi, col0 + h)),
                  pl.BlockSpec((1, t, LANES), lambda b, h, qi: (b, 0, col0 + n_heads + h)),
                  pl.BlockSpec((1, t, LANES), lambda b, h, qi: (b, 0, col0 + 2 * n_heads + h)),
                  pl.BlockSpec(lam_p.shape, lambda b, h, qi: (0, 0)),
                  pl.BlockSpec((1, LANES), lambda b, h, qi: (0, 0))],
        out_specs=pl.BlockSpec((1, blk, LANES), lambda b, h, qi: (b, qi, h)),
        scratch_shapes=[pltpu.VMEM((t, LANES), BF16), pltpu.VMEM((t, LANES), BF16),
                        pltpu.VMEM((2 * blk, blk), F32), pltpu.VMEM((2 * blk, blk), F32),
                        pltpu.VMEM((t // blk, 2 * blk, blk), F32),
                        pltpu.VMEM((2 * blk, LANES), F32), pltpu.VMEM((2 * blk, LANES), F32),
                        pltpu.VMEM((2 * blk, LANES), F32)],
        compiler_params=_params("parallel", "parallel", "arbitrary"),
        name="attn_prompt",
    )(slopes, proj, proj, proj, lam_p, subln_w.reshape(1, LANES))


def _attn_sample_kernel(pt_ref, q_ref, kn_ref, vn_ref, *refs, n_heads, ts, pages_per_step, page, past, lam_init):
    k_refs = refs[:pages_per_step]
    v_refs = refs[pages_per_step:2 * pages_per_step]
    lam_ref, nw_ref, o_ref, q2_ref, base_ref, m_ref, l_ref, acc_ref = refs[2 * pages_per_step:]
    del pt_ref
    step = pl.program_id(1)
    half = LANES // 2
    n_rows = n_heads * 2 * ts
    pcols = page * n_heads
    log_ts = int(math.log2(ts))
    log_h = int(math.log2(n_heads))
    row1 = lax.broadcasted_iota(jnp.int32, (n_rows, 1), 0)
    head = lax.shift_right_logical(row1, log_ts + 1)
    slope = jnp.exp2(-8.0 * (head + 1).astype(F32) / n_heads)
    q_tok = jnp.bitwise_and(row1, ts - 1)

    @pl.when(step == 0)
    def _():
        q = q_ref[0] * (half ** -0.5)
        lane = lax.broadcasted_iota(jnp.int32, (ts, LANES), 1)
        parts = []
        for h in range(n_heads):
            q_h = q[:, h * LANES:(h + 1) * LANES]
            parts += [jnp.where(lane < half, q_h, 0.0), jnp.where(lane >= half, q_h, 0.0)]
        q2_ref[...] = jnp.concatenate(parts, axis=0).astype(BF16)
        col = lax.broadcasted_iota(jnp.int32, (n_rows, pcols), 1)
        col_tok = lax.shift_right_logical(col, log_h)
        base_ref[...] = jnp.where(jnp.bitwise_and(col, n_heads - 1) == head,
                                  -slope * (past + q_tok - col_tok).astype(F32), MASK_VALUE)
        m_ref[...] = jnp.full_like(m_ref, MASK_VALUE)
        l_ref[...] = jnp.zeros_like(l_ref)
        acc_ref[...] = jnp.zeros_like(acc_ref)
        k_new, v_new = kn_ref[0], vn_ref[0]
        k_self = jnp.concatenate([k_new[:, h * LANES:(h + 1) * LANES] for h in range(n_heads)], axis=0)
        v_self = jnp.concatenate([v_new[:, h * LANES:(h + 1) * LANES] for h in range(n_heads)], axis=0)
        n_self = max(LANES, n_heads * ts)
        k_self = _pad_rows(k_self, n_self).astype(BF16)
        v_self = _pad_rows(v_self, n_self).astype(BF16)
        c_self = lax.broadcasted_iota(jnp.int32, (n_rows, n_self), 1)
        c_head = lax.shift_right_logical(c_self, log_ts)
        c_tok = jnp.bitwise_and(c_self, ts - 1)
        s = lax.dot_general(q2_ref[...], k_self, _TRANS_B, preferred_element_type=F32)
        s = jnp.where(c_head == head, jnp.where(c_tok <= q_tok, s - slope * (q_tok - c_tok).astype(F32), MASK_VALUE),
                      MASK_VALUE)
        _softmax_update(s, v_self, m_ref, l_ref, acc_ref)

    q2 = q2_ref[...]
    m_old = m_ref[...]
    m_new = m_old
    s_parts = []
    for ii in range(pages_per_step):
        k_b = k_refs[ii][0, 0].astype(BF16)
        s = lax.dot_general(q2, k_b, _TRANS_B, preferred_element_type=F32) + base_ref[...]
        off = slope * ((step * pages_per_step + ii) * page).astype(F32)
        m_new = jnp.maximum(m_new, jnp.max(s, axis=-1, keepdims=True) + off)
        s_parts.append((s, off))
    alpha = jnp.exp(m_old - m_new)
    l_new = alpha * l_ref[...]
    acc = alpha * acc_ref[...]
    for ii, (s, off) in enumerate(s_parts):
        p = jnp.exp(s - (m_new - off))
        l_new = l_new + jnp.sum(p, axis=-1, keepdims=True)
        acc = acc + jnp.dot(p.astype(BF16), v_refs[ii][0, 0].astype(BF16), preferred_element_type=F32)
    m_ref[...] = m_new
    l_ref[...] = l_new
    acc_ref[...] = acc

    @pl.when(step == pl.num_programs(1) - 1)
    def _():
        lam = _diff_lambda(lam_ref, lam_init)
        nw = nw_ref[...]
        a = acc_ref[...] / l_ref[...]
        for h in range(n_heads):
            r0 = h * 2 * ts
            o_h = a[r0:r0 + ts] - lam * a[r0 + ts:r0 + 2 * ts]
            o_ref[0, :, h * LANES:(h + 1) * LANES] = _rms(o_h, nw) * (1.0 - lam_init)


def _attn_sample(proj, cache_k, cache_v, page_table, lam_p, subln_w, *, layer, n_heads, col0, lam_init):
    bsz, ts, _ = proj.shape
    d = n_heads * LANES
    pcols = cache_k.shape[2]
    page = pcols // n_heads
    n_pages = page_table.shape[1]
    pps = _pick_tile(n_pages, (4, 2, 1))
    assert ts & (ts - 1) == 0 and n_heads & (n_heads - 1) == 0 and cache_k.shape[3] == LANES
    n_rows = n_heads * 2 * ts

    def tok(k):
        return pl.BlockSpec((1, ts, d), lambda b, s, pt, k=k: (b, 0, col0 + k))

    def page_spec(ii):
        return pl.BlockSpec((1, 1, pcols, LANES), lambda b, s, pt, ii=ii: (layer, pt[b, s * pps + ii], 0, 0))

    grid_spec = pltpu.PrefetchScalarGridSpec(
        num_scalar_prefetch=1,
        grid=(bsz, n_pages // pps),
        in_specs=[tok(0), tok(1), tok(2)] + [page_spec(ii) for ii in range(pps)] * 2
        + [pl.BlockSpec(lam_p.shape, lambda b, s, pt: (0, 0)), pl.BlockSpec((1, LANES), lambda b, s, pt: (0, 0))],
        out_specs=pl.BlockSpec((1, ts, d), lambda b, s, pt: (b, 0, 0)),
        scratch_shapes=[pltpu.VMEM((n_rows, LANES), BF16), pltpu.VMEM((n_rows, pcols), F32),
                        pltpu.VMEM((n_rows, 1), F32), pltpu.VMEM((n_rows, 1), F32), pltpu.VMEM((n_rows, LANES), F32)])
    return pl.pallas_call(
        functools.partial(_attn_sample_kernel, n_heads=n_heads, ts=ts, pages_per_step=pps, page=page,
                          past=n_pages * page, lam_init=lam_init),
        out_shape=jax.ShapeDtypeStruct((bsz, ts, d), F32),
        grid_spec=grid_spec,
        compiler_params=_params("parallel", "arbitrary"),
        name="attn_sample",
    )(page_table, proj, proj, proj, *([cache_k] * pps), *([cache_v] * pps), lam_p, subln_w.reshape(1, LANES))


def _ssd_kernel(*refs, hpg, headdim, conv_k, has_state):
    (x_ref, b_ref, c_ref, z_ref, dt_ref, wx_ref, wb_ref, wc_ref, bx_ref, bb_ref, bc_ref,
     dtb_ref, alog_ref, dvec_ref, nw_ref) = refs[:15]
    if has_state:
        cx_ref, cb_ref, cc_ref, s0_ref = refs[15:19]
        refs = refs[19:]
    else:
        refs = refs[15:]
    o_ref, so_ref, st_ref, buf_ref, cumt_ref, dtt_ref = refs
    g = pl.program_id(1)
    t = pl.program_id(2)
    tb = x_ref.shape[1]
    c = SSD_CHUNK
    gw = x_ref.shape[2]
    n_state = b_ref.shape[2]
    assert headdim & (headdim - 1) == 0 and n_state == LANES and tb <= c
    pad = SUBLANES

    @pl.when(t == 0)
    def _():
        if has_state:
            st_ref[...] = s0_ref[0, 0]
            buf_ref[0:pad, 0:gw] = cx_ref[0]
            buf_ref[0:pad, gw:gw + n_state] = cb_ref[0]
            buf_ref[0:pad, gw + n_state:] = cc_ref[0]
        else:
            st_ref[...] = jnp.zeros_like(st_ref)
            buf_ref[0:pad, :] = jnp.zeros((pad, buf_ref.shape[1]), F32)

    buf_ref[pad:pad + c, 0:gw] = _pad_rows(x_ref[0], c)
    buf_ref[pad:pad + c, gw:gw + n_state] = _pad_rows(b_ref[0], c)
    buf_ref[pad:pad + c, gw + n_state:] = _pad_rows(c_ref[0], c)
    w_all = jnp.concatenate([wx_ref[...], wb_ref[...], wc_ref[...]], axis=1)
    conv = jnp.concatenate([bx_ref[...], bb_ref[...], bc_ref[...]], axis=1)
    for j in range(conv_k):
        r0 = pad - (conv_k - 1) + j
        conv = conv + w_all[j:j + 1] * buf_ref[r0:r0 + c, :]
    tail = buf_ref[c:c + pad, :]
    buf_ref[0:pad, :] = tail
    act = _silu(conv)
    xs = act[:, :gw]
    b_c = act[:, gw:gw + n_state].astype(BF16)
    c_c = act[:, gw + n_state:].astype(BF16)

    x_dt = dt_ref[0] + dtb_ref[...]
    dt = jnp.maximum(x_dt, 0.0) + jnp.log1p(jnp.exp(-jnp.abs(x_dt)))
    dt = _pad_rows(dt, c)
    if tb < c:
        dt = jnp.where(lax.broadcasted_iota(jnp.int32, dt.shape, 0) < tb, dt, 0.0)
    a = -jnp.exp(alog_ref[...])
    cum = _cumsum_rows(dt * a)
    cumt_ref[...] = cum.T
    dtt_ref[...] = dt.T
    shift = lax.rem(LANES - g * hpg, LANES)
    cum_g = pltpu.roll(cum, shift, 1)
    dt_g = pltpu.roll(dt, shift, 1)

    cb = lax.dot_general(c_c, b_c, _TRANS_B, preferred_element_type=F32)
    tri = _tril(c)
    log_p = int(math.log2(headdim))
    lane_head = lax.shift_right_logical(lax.broadcasted_iota(jnp.int32, (c, gw), 1), log_p)
    row_head = lax.shift_right_logical(lax.broadcasted_iota(jnp.int32, (gw, n_state), 0), log_p)
    ms = []
    ecum = jnp.zeros((c, gw), F32)
    wsel = jnp.zeros((c, gw), F32)
    dec = jnp.zeros((gw, n_state), F32)
    for r in range(hpg):
        cc = cum_g[:, r:r + 1]
        crow = cumt_ref[pl.ds(g * hpg + r, 1), :]
        drow = dtt_ref[pl.ds(g * hpg + r, 1), :]
        ms.append(jnp.where(tri, cb * jnp.exp(cc - crow) * drow, 0.0).astype(BF16))
        last = cum_g[c - 1:c, r:r + 1]
        ecum = jnp.where(lane_head == r, jnp.exp(cc), ecum)
        wsel = jnp.where(lane_head == r, dt_g[:, r:r + 1] * jnp.exp(last - cc), wsel)
        dec = jnp.where(row_head == r, jnp.exp(last), dec)
    y_all = jnp.dot(jnp.concatenate(ms, axis=0), xs.astype(BF16), preferred_element_type=F32)
    y = jnp.zeros((c, gw), F32)
    for r in range(hpg):
        y = jnp.where(lane_head == r, y_all[r * c:(r + 1) * c], y)
    sp = st_ref[...]
    y = y + lax.dot_general(c_c, sp.astype(BF16), _TRANS_B, preferred_element_type=F32) * ecum
    upd = lax.dot_general((xs * wsel).astype(BF16), b_c, _TRANS_A, preferred_element_type=F32)
    st_ref[...] = dec * sp + upd
    y = y + dvec_ref[...] * xs
    y = y[:tb] * _silu(z_ref[0])
    o_ref[0] = _rms(y, nw_ref[...])

    @pl.when(t == pl.num_programs(2) - 1)
    def _():
        so_ref[0, 0] = st_ref[...]


def _ssd(proj, dtp, conv_w, conv_b, dt_bias, a_log, d_vec, norm_w, conv_state, s0, *, z_col, xbc_col, groups, hpg,
         headdim, n_state):
    bsz, t, _ = proj.shape
    gw = hpg * headdim
    d_inner = groups * gw
    conv_k = conv_w.shape[0]
    tb = SSD_CHUNK if t >= SSD_CHUNK else t
    assert t % tb == 0 and gw % LANES == 0 and n_state == LANES
    assert z_col % gw == 0 and xbc_col % gw == 0 and (xbc_col + d_inner) % n_state == 0
    xb, bb = xbc_col // gw, (xbc_col + d_inner) // n_state
    cb_ = bb + groups
    zb = z_col // gw
    wb0 = d_inner // n_state

    def tok(width, blk0):
        return pl.BlockSpec((1, tb, width), lambda b, g, ti: (b, ti, blk0 + g))

    def par(rows, width, blk0):
        return pl.BlockSpec((rows, width), lambda b, g, ti: (0, blk0 + g))

    def full(shape):
        return pl.BlockSpec(shape, lambda b, g, ti: (0,) * len(shape))

    in_specs = [tok(gw, xb), tok(n_state, bb), tok(n_state, cb_), tok(gw, zb),
                pl.BlockSpec((1, tb, LANES), lambda b, g, ti: (b, ti, 0)),
                par(conv_k, gw, 0), par(conv_k, n_state, wb0), par(conv_k, n_state, wb0 + groups),
                par(1, gw, 0), par(1, n_state, wb0), par(1, n_state, wb0 + groups),
                full((1, LANES)), full((1, LANES)), par(1, gw, 0), par(1, gw, 0)]
    args = [proj, proj, proj, proj, dtp, conv_w, conv_w, conv_w, conv_b, conv_b, conv_b,
            dt_bias, a_log, d_vec, norm_w]
    if s0 is not None:
        def st(width, blk0):
            return pl.BlockSpec((1, SUBLANES, width), lambda b, g, ti: (b, 0, blk0 + g))
        in_specs += [st(gw, 0), st(n_state, wb0), st(n_state, wb0 + groups),
                     pl.BlockSpec((1, 1, gw, n_state), lambda b, g, ti: (b, g, 0, 0))]
        args += [conv_state, conv_state, conv_state, s0]
    return pl.pallas_call(
        functools.partial(_ssd_kernel, hpg=hpg, headdim=headdim, conv_k=conv_k, has_state=s0 is not None),
        out_shape=(jax.ShapeDtypeStruct((bsz, t, d_inner), F32),
                   jax.ShapeDtypeStruct((bsz, groups, gw, n_state), F32)),
        grid=(bsz, groups, t // tb),
        in_specs=in_specs,
        out_specs=(pl.BlockSpec((1, tb, gw), lambda b, g, ti: (b, ti, g)),
                   pl.BlockSpec((1, 1, gw, n_state), lambda b, g, ti: (b, g, 0, 0))),
        scratch_shapes=[pltpu.VMEM((gw, n_state), F32),
                        pltpu.VMEM((SSD_CHUNK + SUBLANES, gw + 2 * n_state), F32),
                        pltpu.VMEM((LANES, SSD_CHUNK), F32), pltpu.VMEM((LANES, SSD_CHUNK), F32)],
        compiler_params=_params("parallel", "parallel", "arbitrary"),
        name="ssd",
    )(*args)


def _merge_kernel(x_ref, oa_ref, ob_ref, yc_ref, g0_ref, g1_ref, g2_ref, wa_ref, wb_ref, wc_ref, wo_ref, o_ref):
    def branch(gate_ref, v_ref, w_ref):
        return jax.nn.sigmoid(gate_ref[...]) * jnp.dot(v_ref[...].astype(BF16), w_ref[...], preferred_element_type=F32)

    merged = branch(g0_ref, oa_ref, wa_ref) + branch(g1_ref, ob_ref, wb_ref) + branch(g2_ref, yc_ref, wc_ref)
    o_ref[...] = x_ref[...] + jnp.dot(merged.astype(BF16), wo_ref[...], preferred_element_type=F32)


def _merge(x, o_a, o_b, y_c, proj, w_a, w_b, w_c, w_o, *, gate_col):
    n, d = x.shape
    tm = _pick_tile(n, (256, 128, 64, 32, 16, 8))
    gb = gate_col // d

    def tok(width, blk=0):
        return pl.BlockSpec((tm, width), lambda i, blk=blk: (i, blk))

    def weight(w):
        return pl.BlockSpec(w.shape, lambda i: (0, 0), pipeline_mode=pl.Buffered(1))

    return pl.pallas_call(
        _merge_kernel,
        out_shape=jax.ShapeDtypeStruct((n, d), F32),
        grid=(n // tm,),
        in_specs=[tok(d), tok(d), tok(d), tok(y_c.shape[1]), tok(d, gb), tok(d, gb + 1), tok(d, gb + 2),
                  weight(w_a), weight(w_b), weight(w_c), weight(w_o)],
        out_specs=tok(d),
        compiler_params=_params("parallel"),
        name="merge",
    )(x, o_a, o_b, y_c, proj, proj, proj, w_a, w_b, w_c, w_o)


def _xattn_kernel(x_ref, mk_ref, mv_ref, g_ref, wq_ref, wo_ref, o_ref, *, n_heads):
    x = x_ref[0]
    d = x.shape[1]
    dh = d // n_heads
    q = jnp.dot(_rms(x, g_ref[...]).astype(BF16), wq_ref[...], preferred_element_type=F32) * (dh ** -0.5)
    outs = []
    for h in range(n_heads):
        cols = slice(h * dh, (h + 1) * dh)
        k_b = mk_ref[0, :, cols].astype(BF16)
        v_b = mv_ref[0, :, cols].astype(BF16)
        s = lax.dot_general(q[:, cols].astype(BF16), k_b, _TRANS_B, preferred_element_type=F32)
        p = jnp.exp(s - jnp.max(s, axis=-1, keepdims=True))
        o_h = jnp.dot(p.astype(BF16), v_b, preferred_element_type=F32)
        outs.append(o_h / jnp.sum(p, axis=-1, keepdims=True))
    o_m = jnp.concatenate(outs, axis=1).astype(BF16)
    o_ref[0] = x + jnp.dot(o_m, wo_ref[...], preferred_element_type=F32)


def _xattn(x, mem_k, mem_v, g, w_q, w_o, *, n_heads):
    bsz, t, d = x.shape
    m = mem_k.shape[1]
    tq = _pick_tile(t, (512, 256, 128, 64, 32, 16, 8))

    def weight(w):
        return pl.BlockSpec(w.shape, lambda b, ti: (0, 0), pipeline_mode=pl.Buffered(1))

    return pl.pallas_call(
        functools.partial(_xattn_kernel, n_heads=n_heads),
        out_shape=jax.ShapeDtypeStruct((bsz, t, d), F32),
        grid=(bsz, t // tq),
        in_specs=[pl.BlockSpec((1, tq, d), lambda b, ti: (b, ti, 0)),
                  pl.BlockSpec((1, m, d), lambda b, ti: (b, 0, 0)),
                  pl.BlockSpec((1, m, d), lambda b, ti: (b, 0, 0)),
                  pl.BlockSpec((1, d), lambda b, ti: (0, 0)),
                  weight(w_q), weight(w_o)],
        out_specs=pl.BlockSpec((1, tq, d), lambda b, ti: (b, ti, 0)),
        compiler_params=_params("parallel", "parallel"),
        name="xattn",
    )(x, mem_k, mem_v, g.reshape(1, d), w_q, w_o)


def _ffn_kernel(*refs, conv_k, has_state, final_norm):
    x_ref, g_ref, wg_ref, wv_ref, wd_ref, cw_ref, cb_ref = refs[:7]
    refs = refs[7:]
    if has_state:
        cs_ref, refs = refs[0], refs[1:]
    if final_norm:
        fn_ref, refs = refs[0], refs[1:]
    o_ref, fo_ref, hn_ref, acc_ref, buf_ref, tail_ref = refs
    t = pl.program_id(1)
    f = pl.program_id(2)
    tt = x_ref.shape[1]
    pad = SUBLANES

    @pl.when(f == 0)
    def _():
        hn_ref[...] = _rms(x_ref[0], g_ref[...]).astype(BF16)
        acc_ref[...] = jnp.zeros_like(acc_ref)

    @pl.when(t == 0)
    def _():
        if has_state:
            tail_ref[f] = cs_ref[0]
        else:
            tail_ref[f] = jnp.zeros(tail_ref.shape[1:], F32)

    hn = hn_ref[...]
    gate = jnp.dot(hn, wg_ref[...], preferred_element_type=F32)
    val = jnp.dot(hn, wv_ref[...], preferred_element_type=F32)
    buf_ref[0:pad, :] = tail_ref[f]
    buf_ref[pad:pad + tt, :] = gate
    cw = cw_ref[...]
    conv = cb_ref[...]
    for j in range(conv_k):
        r0 = pad - (conv_k - 1) + j
        conv = conv + cw[j:j + 1] * buf_ref[r0:r0 + tt, :]
    tail_ref[f] = buf_ref[tt:tt + pad, :]
    fo_ref[0, 0] = buf_ref[pad + tt - (conv_k - 1):pad + tt, :]
    acc_ref[...] += jnp.dot((_silu(conv) * val).astype(BF16), wd_ref[...], preferred_element_type=F32)

    @pl.when(f == pl.num_programs(2) - 1)
    def _():
        y = x_ref[0] + acc_ref[...]
        if final_norm:
            y = _rms(y, fn_ref[...])
        o_ref[0] = y


def _ffn(x, g, w_up, w_down, conv_w, conv_b, conv_state, final_w):
    bsz, t, d = x.shape
    ff = w_down.shape[0]
    conv_k = conv_w.shape[0]
    tt = _pick_tile(t, (512, 256, 128, 64, 32, 16, 8))
    tf = _pick_tile(ff, (1408, 1024, 512, 256, 128))
    nf = ff // tf
    assert tt >= conv_k - 1
    in_specs = [pl.BlockSpec((1, tt, d), lambda b, ti, f: (b, ti, 0)),
                pl.BlockSpec((1, d), lambda b, ti, f: (0, 0)),
                pl.BlockSpec((d, tf), lambda b, ti, f: (0, f)),
                pl.BlockSpec((d, tf), lambda b, ti, f: (0, nf + f)),
                pl.BlockSpec((tf, d), lambda b, ti, f: (f, 0)),
                pl.BlockSpec((conv_k, tf), lambda b, ti, f: (0, f)),
                pl.BlockSpec((1, tf), lambda b, ti, f: (0, f))]
    args = [x, g.reshape(1, d), w_up, w_up, w_down, conv_w, conv_b.reshape(1, ff)]
    if conv_state is not None:
        in_specs.append(pl.BlockSpec((1, SUBLANES, tf), lambda b, ti, f: (b, 0, f)))
        args.append(conv_state)
    if final_w is not None:
        in_specs.append(pl.BlockSpec((1, d), lambda b, ti, f: (0, 0)))
        args.append(final_w.reshape(1, d))
    y, tails = pl.pallas_call(
        functools.partial(_ffn_kernel, conv_k=conv_k, has_state=conv_state is not None,
                          final_norm=final_w is not None),
        out_shape=(jax.ShapeDtypeStruct((bsz, t, d), F32),
                   jax.ShapeDtypeStruct((bsz, t // tt, conv_k - 1, ff), F32)),
        grid=(bsz, t // tt, nf),
        in_specs=in_specs,
        out_specs=(pl.BlockSpec((1, tt, d), lambda b, ti, f: (b, ti, 0)),
                   pl.BlockSpec((1, 1, conv_k - 1, tf), lambda b, ti, f: (b, ti, 0, f))),
        scratch_shapes=[pltpu.VMEM((tt, d), BF16), pltpu.VMEM((tt, d), F32),
                        pltpu.VMEM((tt + SUBLANES, tf), F32), pltpu.VMEM((nf, SUBLANES, tf), F32)],
        compiler_params=_params("parallel", "arbitrary", "arbitrary"),
        name="ffn",
    )(*args)
    return y, tails[:, -1]


def _pad_state_rows(s):
    return jnp.pad(s, ((0, 0), (SUBLANES - s.shape[1], 0), (0, 0)))


def _pad_lanes(v):
    return jnp.pad(v, (0, LANES - v.shape[0])).reshape(1, LANES)


def kernel(x_prompt, x_sample, cache_k, cache_v, cache_mem_k, cache_mem_v, state_hgrn, state_ssm, state_conv, state_ffn_conv, page_table, mem_prompt, norm_mix, w_in, hgrn_lb_logits, hgrn_out_norm, diff_lambda, diff_subln, ssm_conv_w, ssm_conv_b, ssm_dt_bias, ssm_a_log, ssm_d, ssm_norm, w_br_a, w_br_b, w_br_c, w_out, norm_mem_q, norm_mem_kv, w_mq, w_mk, w_mv, w_mo, norm_ffn, w_up, ffn_conv_w, ffn_conv_b, w_down, norm_final):
    depth = w_in.shape[0]
    bp, tp, d = x_prompt.shape
    bs, ts, _ = x_sample.shape
    n_heads = d // LANES
    groups, hpg, headdim, n_state = state_ssm.shape[2:]
    d_inner = groups * hpg * headdim
    conv_dim = state_conv.shape[-1]
    ssm_heads = groups * hpg
    mem_len, mem_heads = cache_mem_k.shape[2], cache_mem_k.shape[3]
    n_phys, page = cache_k.shape[1], cache_k.shape[2]
    assert ssm_heads <= LANES and conv_dim == d_inner + 2 * groups * n_state
    z_col = 7 * d
    xbc_col = z_col + d_inner
    dt_col = xbc_col + conv_dim
    gate_col = dt_col

    slopes = jnp.exp2(-8.0 * jnp.arange(1, n_heads + 1, dtype=F32) / n_heads)
    cache_k2 = cache_k.reshape(depth, n_phys, page * n_heads, LANES)
    cache_v2 = cache_v.reshape(depth, n_phys, page * n_heads, LANES)

    xp, xs = x_prompt, x_sample
    outs = [[] for _ in range(14)]
    for l in range(depth):
        lam_init = 0.8 - 0.6 * math.exp(-0.3 * l)
        last = l == depth - 1
        w_main = jnp.concatenate([w_in[l][:, :dt_col], w_in[l][:, dt_col + ssm_heads:]], axis=1).astype(BF16)
        w_dt = jnp.pad(w_in[l][:, dt_col:dt_col + ssm_heads], ((0, 0), (0, LANES - ssm_heads))).astype(BF16)
        wa, wb, wc, wo = (w[l].astype(BF16) for w in (w_br_a, w_br_b, w_br_c, w_out))
        wq, wk, wv, wmo = (w[l].astype(BF16) for w in (w_mq, w_mk, w_mv, w_mo))
        wup, wdn = w_up[l].astype(BF16), w_down[l].astype(BF16)
        dt_bias, a_log = _pad_lanes(ssm_dt_bias[l]), _pad_lanes(ssm_a_log[l])
        d_vec = jnp.repeat(ssm_d[l], headdim).reshape(1, d_inner)
        ssm_nw = ssm_norm[l].reshape(1, d_inner)
        conv_b = ssm_conv_b[l].reshape(1, conv_dim)

        mem_n = mem_prompt.reshape(bp * mem_len, d)
        mk = _norm_matmul(mem_n, norm_mem_kv[l], wk).reshape(bp, mem_len, d)
        mv = _norm_matmul(mem_n, norm_mem_kv[l], wv).reshape(bp, mem_len, d)

        def trunk(x, mem_k, mem_v, hgrn_s0, ssm_s0, conv_s, ffn_s, attend):
            bsz, t, _ = x.shape
            x2 = x.reshape(bsz * t, d)
            proj2 = _norm_matmul(x2, norm_mix[l], w_main)
            proj = proj2.reshape(bsz, t, -1)
            dtp = _norm_matmul(x2, norm_mix[l], w_dt).reshape(bsz, t, LANES)
            o_a, s_a = _hgrn(proj, hgrn_lb_logits, hgrn_out_norm[l], hgrn_s0, layer=l, n_heads=n_heads, col0=0)
            o_b = attend(proj)
            y_c, s_c = _ssd(proj, dtp, ssm_conv_w[l], conv_b, dt_bias, a_log, d_vec, ssm_nw, conv_s,
                            None if ssm_s0 is None else ssm_s0.reshape(bsz, groups, hpg * headdim, n_state),
                            z_col=z_col, xbc_col=xbc_col, groups=groups, hpg=hpg, headdim=headdim, n_state=n_state)
            x2 = _merge(x2, o_a.reshape(bsz * t, d), o_b.reshape(bsz * t, d), y_c.reshape(bsz * t, d_inner), proj2,
                        wa, wb, wc, wo, gate_col=gate_col)
            x3 = _xattn(x2.reshape(bsz, t, d), mem_k, mem_v, norm_mem_q[l], wq, wmo, n_heads=mem_heads)
            x3, ffn_new = _ffn(x3, norm_ffn[l], wup, wdn, ffn_conv_w[l], ffn_conv_b[l], ffn_s,
                               norm_final if last else None)
            k_new = proj[:, :, 5 * d:6 * d].reshape(bsz, t, n_heads, LANES)
            v_new = proj[:, :, 6 * d:7 * d].reshape(bsz, t, n_heads, LANES)
            conv_new = proj[:, t - (ssm_conv_w.shape[1] - 1):, xbc_col:xbc_col + conv_dim]
            return x3, (k_new, v_new, s_a, s_c.reshape(bsz, groups, hpg, headdim, n_state), conv_new, ffn_new)

        xp, st_p = trunk(
            xp, mk, mv, None, None, None, None,
            lambda proj: _attn_prompt(proj, slopes, diff_lambda[l], diff_subln[l], n_heads=n_heads,
                                      col0=4 * n_heads, lam_init=lam_init))
        xs, st_s = trunk(
            xs, cache_mem_k[l].reshape(bs, mem_len, d), cache_mem_v[l].reshape(bs, mem_len, d),
            state_hgrn[l], state_ssm[l], _pad_state_rows(state_conv[l]), _pad_state_rows(state_ffn_conv[l]),
            lambda proj: _attn_sample(proj, cache_k2, cache_v2, page_table, diff_lambda[l], diff_subln[l],
                                      layer=l, n_heads=n_heads, col0=4, lam_init=lam_init))
        new = (st_p[0], st_p[1], st_s[0], st_s[1],
               mk.reshape(bp, mem_len, mem_heads, d // mem_heads), mv.reshape(bp, mem_len, mem_heads, d // mem_heads),
               st_p[2], st_s[2], st_p[3], st_s[3], st_p[4], st_s[4], st_p[5], st_s[5])
        for acc, val in zip(outs, new):
            acc.append(val)
    return (xp, xs) + tuple(jnp.stack(o) for o in outs)
```

```python
import functools
import math

import jax
import jax.numpy as jnp
from jax import lax
from jax.experimental import pallas as pl
from jax.experimental.pallas import tpu as pltpu

F32 = jnp.float32
BF16 = jnp.bfloat16
EPS = 1e-6
MASK_VALUE = -1e30
LANES = 128
SUBLANES = 8
HGRN_CHUNK = 64
HGRN_HEADS_PER_STEP = 2
SSD_GROUPS_PER_STEP = 4
PAGES_PER_STEP = 8
SSD_CHUNK = 128
VMEM_LIMIT = 48 * 1024 * 1024

_TRANS_B = (((1,), (1,)), ((), ()))
_TRANS_A = (((0,), (0,)), ((), ()))


def _params(*sem):
    return pltpu.CompilerParams(dimension_semantics=sem, vmem_limit_bytes=VMEM_LIMIT)


def _silu(x):
    return x * jax.nn.sigmoid(x)


def _rms(x, w):
    return x * lax.rsqrt(jnp.mean(x * x, axis=-1, keepdims=True) + EPS) * w


def _pad_rows(x, rows):
    if x.shape[0] == rows:
        return x
    return jnp.concatenate([x, jnp.zeros((rows - x.shape[0],) + x.shape[1:], x.dtype)], axis=0)


def _tril(n):
    r = lax.broadcasted_iota(jnp.int32, (n, n), 0)
    c = lax.broadcasted_iota(jnp.int32, (n, n), 1)
    return r >= c


def _select_sum_rows(w01, x):
    n = x.shape[1]
    hi = x.astype(BF16)
    r1 = x - hi.astype(F32)
    mid = r1.astype(BF16)
    lo = (r1 - mid.astype(F32)).astype(BF16)
    y = jnp.dot(w01.astype(BF16), jnp.concatenate([hi, mid, lo], axis=1), preferred_element_type=F32)
    return y[:, :n] + y[:, n:2 * n] + y[:, 2 * n:]


def _cumsum_rows(x):
    return _select_sum_rows(_tril(x.shape[0]).astype(F32), x)


def _pick_tile(n, candidates):
    for c in candidates:
        if n % c == 0:
            return c
    return n


def _norm_matmul_kernel(x_ref, g_ref, w_ref, o_ref, xn_ref):
    @pl.when(pl.program_id(1) == 0)
    def _():
        xn_ref[...] = _rms(x_ref[...], g_ref[...]).astype(BF16)

    o_ref[...] = jnp.dot(xn_ref[...], w_ref[...], preferred_element_type=F32)


def _norm_matmul(x, g, w):
    n, d = x.shape
    m = w.shape[1]
    tm = _pick_tile(n, (1024, 512, 256, 128, 64, 32, 16, 8))
    tn = _pick_tile(m, (1024, 512, 256, 128))
    return pl.pallas_call(
        _norm_matmul_kernel,
        out_shape=jax.ShapeDtypeStruct((n, m), F32),
        grid=(n // tm, m // tn),
        in_specs=[pl.BlockSpec((tm, d), lambda i, j: (i, 0)),
                  pl.BlockSpec((1, d), lambda i, j: (0, 0)),
                  pl.BlockSpec((d, tn), lambda i, j: (0, j))],
        out_specs=pl.BlockSpec((tm, tn), lambda i, j: (i, j)),
        scratch_shapes=[pltpu.VMEM((tm, d), BF16)],
        compiler_params=_params("parallel", "arbitrary"),
        name="norm_matmul",
    )(x, g.reshape(1, d), w)


def _cumsum_and_midpoints(x, levels):
    c = x.shape[0]
    sub = jnp.bitwise_and(lax.broadcasted_iota(jnp.int32, x.shape, 0), SUBLANES - 1)
    for d in (1, 2, 4):
        x = x + jnp.where(sub >= d, pltpu.roll(x, d, 0), 0.0)
    groups, total = [], None
    for v in range(c // SUBLANES):
        grp = x[v * SUBLANES:(v + 1) * SUBLANES]
        if total is not None:
            grp = grp + total
        total = grp[SUBLANES - 1:SUBLANES]
        groups.append(grp)
    b = jnp.concatenate(groups, axis=0)
    mids = []
    for m in levels:
        if m == 1:
            mids.append(jnp.where(jnp.bitwise_and(sub, 1) == 1, pltpu.roll(b, 1, 0), b))
        elif m == 2:
            s4 = jnp.bitwise_and(sub, 3)
            mids.append(jnp.where(s4 == 0, pltpu.roll(b, c - 1, 0),
                                  jnp.where(s4 == 1, b, jnp.where(s4 == 2, pltpu.roll(b, 1, 0), pltpu.roll(b, 2, 0)))))
        else:
            mids.append(jnp.concatenate(
                [jnp.broadcast_to(b[blk * 2 * m + m - 1:blk * 2 * m + m], (2 * m, x.shape[1]))
                 for blk in range(c // (2 * m))], axis=0))
    return b, mids


def _hgrn_kernel(*refs, layer, has_s0):
    if has_s0:
        q_ref, f_ref, i_ref, g_ref, lb_ref, nw_ref, s0_ref, o_ref, so_ref, st_ref = refs
    else:
        q_ref, f_ref, i_ref, g_ref, lb_ref, nw_ref, o_ref, so_ref, st_ref = refs
    t = pl.program_id(2)
    tb = q_ref.shape[1]
    c = HGRN_CHUNK
    n_chunks = max(1, tb // c)
    rows = min(tb, c)
    hps = st_ref.shape[0]

    @pl.when(t == 0)
    def _():
        for hh in range(hps):
            if has_s0:
                st_ref[hh] = s0_ref[0, hh].T
            else:
                st_ref[hh] = jnp.zeros((LANES, LANES), F32)

    lbl = lb_ref[...]
    e = jnp.exp(lbl - jnp.max(lbl, axis=0, keepdims=True))
    soft = e / jnp.sum(e, axis=0, keepdims=True)
    lb = jnp.sum(soft[:layer + 1], axis=0, keepdims=True) - soft[0:1]

    nw = nw_ref[...]
    row_id = lax.broadcasted_iota(jnp.int32, (c, LANES), 0)
    assert 2 * c == LANES
    levels = [1 << j for j in range(int(math.log2(c)))]
    upper = [jnp.bitwise_and(row_id, m) != 0 for m in levels]
    group = 4
    slots = [None] + levels
    slots += [0] * (-len(slots) % group)
    lane_id = lax.broadcasted_iota(jnp.int32, (c, LANES), 1)
    key = jnp.bitwise_and(lane_id, c - 1)
    lane_half = lax.shift_right_logical(lane_id, int(math.log2(c)))
    slot_mask = []
    for si, m in enumerate(slots):
        in_half = lane_half == (si % 2)
        if m is None:
            slot_mask.append(jnp.logical_and(in_half, row_id == key))
        elif m > 0:
            slot_mask.append(jnp.logical_and(in_half, jnp.bitwise_and(row_id, -2 * m) == jnp.bitwise_and(key, -2 * m)))
        else:
            slot_mask.append(None)

    chunks = []
    for hh in range(hps):
        hl = slice(hh * LANES, (hh + 1) * LANES)
        for ci in range(n_chunks):
            sl = slice(ci * c, ci * c + rows)
            f = _pad_rows(f_ref[0, sl, hl], c)
            q = _silu(_pad_rows(q_ref[0, sl, hl], c))
            v = _pad_rows(i_ref[0, sl, hl], c)
            forget = lb[:, hl] + (1.0 - lb[:, hl]) * jax.nn.sigmoid(f)
            logf = jnp.log(forget)
            kk = 1.0 - forget
            if rows < c:
                logf = jnp.where(row_id < rows, logf, 0.0)
                kk = jnp.where(row_id < rows, kk, 0.0)
            chunks.append((hh, hl, sl, q, v, kk, logf))
    states = [st_ref[hh] for hh in range(hps)]
    zero_slot = jnp.zeros((c, LANES), BF16)
    for idx, (hh, hl, sl, q, v, kk, logf) in enumerate(chunks):
        st = states[hh]
        b, brefs = _cumsum_and_midpoints(logf, levels)
        v_b = v.astype(BF16)
        q_parts, k_parts = [q.astype(BF16)], [kk.astype(BF16)]
        for li in range(len(levels)):
            bref = brefs[li]
            q_parts.append(jnp.where(upper[li], q * jnp.exp(b - bref), 0.0).astype(BF16))
            k_parts.append(jnp.where(upper[li], 0.0, kk * jnp.exp(bref - b)).astype(BF16))
        q_parts += [zero_slot] * (len(slots) - len(q_parts))
        k_parts += [zero_slot] * (len(slots) - len(k_parts))
        scores = jnp.zeros((c, LANES), F32)
        for g0 in range(0, len(slots), group):
            s_g = lax.dot_general(jnp.concatenate(q_parts[g0:g0 + group], axis=0),
                                  jnp.concatenate(k_parts[g0:g0 + group], axis=0), _TRANS_B,
                                  preferred_element_type=F32)
            for i in range(group):
                if slot_mask[g0 + i] is not None:
                    tile = s_g[i * c:(i + 1) * c, (i // 2) * LANES:(i // 2 + 1) * LANES]
                    scores = scores + jnp.where(slot_mask[g0 + i], tile, 0.0)
        o = lax.dot_general((q * jnp.exp(b)).astype(BF16), st.astype(BF16), _TRANS_B, preferred_element_type=F32)
        o = o + jnp.dot(scores.astype(BF16), jnp.concatenate([v_b, v_b], axis=0), preferred_element_type=F32)
        last = b[c - 1:c]
        kdec = (kk * jnp.exp(last - b)).astype(BF16)
        states[hh] = jnp.exp(last) * st + lax.dot_general(v_b, kdec, _TRANS_A, preferred_element_type=F32)
        g = g_ref[0, sl, hl]
        o_ref[0, sl, hl] = _rms(o[:rows], nw) * _silu(g)
    for hh in range(hps):
        st_ref[hh] = states[hh]

    @pl.when(t == pl.num_programs(2) - 1)
    def _():
        for hh in range(hps):
            so_ref[0, hh] = states[hh].T


def _hgrn(proj, lb_logits, norm_w, s0, *, layer, n_heads, col0):
    bsz, t, _ = proj.shape
    tb = _pick_tile(t, (512, 256, 128, 64)) if t >= HGRN_CHUNK else t
    depth = lb_logits.shape[0]
    hps = HGRN_HEADS_PER_STEP if n_heads % HGRN_HEADS_PER_STEP == 0 and col0 % HGRN_HEADS_PER_STEP == 0 else 1
    width = hps * LANES

    def col(k):
        return pl.BlockSpec((1, tb, width), lambda b, h, ti, k=k: (b, ti, (col0 + k * n_heads) // hps + h))

    in_specs = [col(0), col(1), col(2), col(3),
                pl.BlockSpec((depth, width), lambda b, h, ti: (0, h)),
                pl.BlockSpec((1, LANES), lambda b, h, ti: (0, 0))]
    args = [proj, proj, proj, proj, lb_logits, norm_w.reshape(1, LANES)]
    if s0 is not None:
        in_specs.append(pl.BlockSpec((1, hps, LANES, LANES), lambda b, h, ti: (b, h, 0, 0)))
        args.append(s0)
    return pl.pallas_call(
        functools.partial(_hgrn_kernel, layer=layer, has_s0=s0 is not None),
        out_shape=(jax.ShapeDtypeStruct((bsz, t, n_heads * LANES), F32),
                   jax.ShapeDtypeStruct((bsz, n_heads, LANES, LANES), F32)),
        grid=(bsz, n_heads // hps, t // tb),
        in_specs=in_specs,
        out_specs=(pl.BlockSpec((1, tb, width), lambda b, h, ti: (b, ti, h)),
                   pl.BlockSpec((1, hps, LANES, LANES), lambda b, h, ti: (b, h, 0, 0))),
        scratch_shapes=[pltpu.VMEM((hps, LANES, LANES), F32)],
        compiler_params=_params("parallel", "parallel", "arbitrary"),
        name="hgrn",
    )(*args)


def _diff_lambda(lam_ref, lam_init):
    l4 = lam_ref[...]
    return (jnp.exp(jnp.sum(l4[0:1] * l4[1:2], axis=-1, keepdims=True))
            - jnp.exp(jnp.sum(l4[2:3] * l4[3:4], axis=-1, keepdims=True)) + lam_init)


def _softmax_update(s, v_b, m_ref, l_ref, acc_ref):
    m_old = m_ref[...]
    m_new = jnp.maximum(m_old, jnp.max(s, axis=-1, keepdims=True))
    alpha = jnp.exp(m_old - m_new)
    p = jnp.exp(s - m_new)
    l_ref[...] = alpha * l_ref[...] + jnp.sum(p, axis=-1, keepdims=True)
    acc_ref[...] = alpha * acc_ref[...] + jnp.dot(p.astype(BF16), v_b, preferred_element_type=F32)
    m_ref[...] = m_new


def _attn_prompt_kernel(slope_ref, q_ref, k_ref, v_ref, lam_ref, nw_ref, o_ref, kb_ref, vb_ref, bias_ref, bdiag_ref,
                        s_ref, m_ref, l_ref, acc_ref, *, blk, lam_init):
    h = pl.program_id(1)
    qi = pl.program_id(2)
    half = LANES // 2
    n_sub = blk // LANES
    neg_slope = -slope_ref[h]

    @pl.when(qi == 0)
    def _():
        kb_ref[...] = k_ref[0].astype(BF16)
        vb_ref[...] = v_ref[0].astype(BF16)
        r = lax.broadcasted_iota(jnp.int32, (2 * blk, blk), 0)
        r = jnp.where(r >= blk, r - blk, r)
        c = lax.broadcasted_iota(jnp.int32, (2 * blk, blk), 1)
        bias0 = neg_slope * (r - c).astype(F32)
        bias_ref[...] = bias0
        bdiag_ref[...] = jnp.where(c <= r, bias0, MASK_VALUE)

    q = q_ref[0] * (half ** -0.5)
    lane = lax.broadcasted_iota(jnp.int32, q.shape, 1)
    qs = jnp.concatenate([jnp.where(lane < half, q, 0.0), jnp.where(lane >= half, q, 0.0)], axis=0).astype(BF16)

    def scores(j, bias):
        start = pl.multiple_of(j * blk, blk)
        s = lax.dot_general(qs, kb_ref[pl.ds(start, blk), :], _TRANS_B, preferred_element_type=F32) + bias
        s_ref[j] = s
        m = m_ref[...]
        for u in range(n_sub):
            m = jnp.maximum(m, s[:, u * LANES:(u + 1) * LANES])
        m_ref[...] = m

    m_ref[...] = jnp.full_like(m_ref, MASK_VALUE)

    def pass1(j, carry):
        scores(j, bias_ref[...] + neg_slope * ((qi - j) * blk).astype(F32))
        return carry

    lax.fori_loop(0, qi, pass1, 0)
    scores(qi, bdiag_ref[...])
    m_row = jnp.broadcast_to(jnp.max(m_ref[...], axis=-1, keepdims=True), m_ref.shape)

    l_ref[...] = jnp.zeros_like(l_ref)
    acc_ref[...] = jnp.zeros_like(acc_ref)

    def pass2(j, carry):
        start = pl.multiple_of(j * blk, blk)
        s = s_ref[j]
        l = l_ref[...]
        ps = []
        for u in range(n_sub):
            p = jnp.exp(s[:, u * LANES:(u + 1) * LANES] - m_row)
            l = l + p
            ps.append(p.astype(BF16))
        l_ref[...] = l
        acc_ref[...] += jnp.dot(jnp.concatenate(ps, axis=1), vb_ref[pl.ds(start, blk), :], preferred_element_type=F32)
        return carry

    lax.fori_loop(0, qi + 1, pass2, 0)

    lam = _diff_lambda(lam_ref, lam_init)
    acc = acc_ref[...]
    l = jnp.sum(l_ref[...], axis=-1, keepdims=True)
    o = acc[:blk] / l[:blk] - lam * (acc[blk:] / l[blk:])
    o_ref[0] = _rms(o, nw_ref[...]) * (1.0 - lam_init)


def _attn_prompt(proj, slopes, lam_p, subln_w, *, n_heads, col0, lam_init):
    bsz, t, _ = proj.shape
    blk = _pick_tile(t, (512, 256, 128))
    return pl.pallas_call(
        functools.partial(_attn_prompt_kernel, blk=blk, lam_init=lam_init),
        out_shape=jax.ShapeDtypeStruct((bsz, t, n_heads * LANES), F32),
        grid=(bsz, n_heads, t // blk),
        in_specs=[pl.BlockSpec(memory_space=pltpu.SMEM),
                  pl.BlockSpec((1, blk, LANES), lambda b, h, qi: (b, qi, col0 + h)),
                  pl.BlockSpec((1, t, LANES), lambda b, h, qi: (b, 0, col0 + n_heads + h)),
                  pl.BlockSpec((1, t, LANES), lambda b, h, qi: (b, 0, col0 + 2 * n_heads + h)),
                  pl.BlockSpec(lam_p.shape, lambda b, h, qi: (0, 0)),
                  pl.BlockSpec((1, LANES), lambda b, h, qi: (0, 0))],
        out_specs=pl.BlockSpec((1, blk, LANES), lambda b, h, qi: (b, qi, h)),
        scratch_shapes=[pltpu.VMEM((t, LANES), BF16), pltpu.VMEM((t, LANES), BF16),
                        pltpu.VMEM((2 * blk, blk), F32), pltpu.VMEM((2 * blk, blk), F32),
                        pltpu.VMEM((t // blk, 2 * blk, blk), F32),
                        pltpu.VMEM((2 * blk, LANES), F32), pltpu.VMEM((2 * blk, LANES), F32),
                        pltpu.VMEM((2 * blk, LANES), F32)],
        compiler_params=_params("parallel", "parallel", "arbitrary"),
        name="attn_prompt",
    )(slopes, proj, proj, proj, lam_p, subln_w.reshape(1, LANES))


def _attn_sample_kernel(pt_ref, q_ref, kn_ref, vn_ref, *refs, n_heads, ts, pages_per_step, page, past, lam_init):
    k_refs = refs[:pages_per_step]
    v_refs = refs[pages_per_step:2 * pages_per_step]
    lam_ref, nw_ref, o_ref, q2_ref, base_ref, m_ref, l_ref, acc_ref = refs[2 * pages_per_step:]
    del pt_ref
    step = pl.program_id(1)
    half = LANES // 2
    n_rows = n_heads * 2 * ts
    pcols = page * n_heads
    log_ts = int(math.log2(ts))
    log_h = int(math.log2(n_heads))
    row1 = lax.broadcasted_iota(jnp.int32, (n_rows, 1), 0)
    head = lax.shift_right_logical(row1, log_ts + 1)
    slope = jnp.exp2(-8.0 * (head + 1).astype(F32) / n_heads)
    q_tok = jnp.bitwise_and(row1, ts - 1)

    @pl.when(step == 0)
    def _():
        q = q_ref[0] * (half ** -0.5)
        lane = lax.broadcasted_iota(jnp.int32, (ts, LANES), 1)
        parts = []
        for h in range(n_heads):
            q_h = q[:, h * LANES:(h + 1) * LANES]
            parts += [jnp.where(lane < half, q_h, 0.0), jnp.where(lane >= half, q_h, 0.0)]
        q2_ref[...] = jnp.concatenate(parts, axis=0).astype(BF16)
        col = lax.broadcasted_iota(jnp.int32, (n_rows, pcols), 1)
        col_tok = lax.shift_right_logical(col, log_h)
        base_ref[...] = jnp.where(jnp.bitwise_and(col, n_heads - 1) == head,
                                  -slope * (past + q_tok - col_tok).astype(F32), MASK_VALUE)
        m_ref[...] = jnp.full_like(m_ref, MASK_VALUE)
        l_ref[...] = jnp.zeros_like(l_ref)
        acc_ref[...] = jnp.zeros_like(acc_ref)
        k_new, v_new = kn_ref[0], vn_ref[0]
        k_self = jnp.concatenate([k_new[:, h * LANES:(h + 1) * LANES] for h in range(n_heads)], axis=0)
        v_self = jnp.concatenate([v_new[:, h * LANES:(h + 1) * LANES] for h in range(n_heads)], axis=0)
        n_self = max(LANES, n_heads * ts)
        k_self = _pad_rows(k_self, n_self).astype(BF16)
        v_self = _pad_rows(v_self, n_self).astype(BF16)
        c_self = lax.broadcasted_iota(jnp.int32, (n_rows, n_self), 1)
        c_head = lax.shift_right_logical(c_self, log_ts)
        c_tok = jnp.bitwise_and(c_self, ts - 1)
        s = lax.dot_general(q2_ref[...], k_self, _TRANS_B, preferred_element_type=F32)
        s = jnp.where(c_head == head, jnp.where(c_tok <= q_tok, s - slope * (q_tok - c_tok).astype(F32), MASK_VALUE),
                      MASK_VALUE)
        _softmax_update(s, v_self, m_ref, l_ref, acc_ref)

    q2 = q2_ref[...]
    m_old = m_ref[...]
    m_new = m_old
    s_parts = []
    for ii in range(pages_per_step):
        k_b = k_refs[ii][0, 0].astype(BF16)
        s = lax.dot_general(q2, k_b, _TRANS_B, preferred_element_type=F32) + base_ref[...]
        off = slope * ((step * pages_per_step + ii) * page).astype(F32)
        m_new = jnp.maximum(m_new, jnp.max(s, axis=-1, keepdims=True) + off)
        s_parts.append((s, off))
    alpha = jnp.exp(m_old - m_new)
    l_new = alpha * l_ref[...]
    acc = alpha * acc_ref[...]
    for ii, (s, off) in enumerate(s_parts):
        p = jnp.exp(s - (m_new - off))
        l_new = l_new + jnp.sum(p, axis=-1, keepdims=True)
        acc = acc + jnp.dot(p.astype(BF16), v_refs[ii][0, 0].astype(BF16), preferred_element_type=F32)
    m_ref[...] = m_new
    l_ref[...] = l_new
    acc_ref[...] = acc

    @pl.when(step == pl.num_programs(1) - 1)
    def _():
        lam = _diff_lambda(lam_ref, lam_init)
        nw = nw_ref[...]
        a = acc_ref[...] / l_ref[...]
        for h in range(n_heads):
            r0 = h * 2 * ts
            o_h = a[r0:r0 + ts] - lam * a[r0 + ts:r0 + 2 * ts]
            o_ref[0, :, h * LANES:(h + 1) * LANES] = _rms(o_h, nw) * (1.0 - lam_init)


def _attn_sample(proj, cache_k, cache_v, page_table, lam_p, subln_w, *, layer, n_heads, col0, lam_init):
    bsz, ts, _ = proj.shape
    d = n_heads * LANES
    pcols = cache_k.shape[2]
    page = pcols // n_heads
    n_pages = page_table.shape[1]
    pps = _pick_tile(n_pages, (PAGES_PER_STEP, 4, 2, 1))
    assert ts & (ts - 1) == 0 and n_heads & (n_heads - 1) == 0 and cache_k.shape[3] == LANES
    n_rows = n_heads * 2 * ts

    def tok(k):
        return pl.BlockSpec((1, ts, d), lambda b, s, pt, k=k: (b, 0, col0 + k))

    def page_spec(ii):
        return pl.BlockSpec((1, 1, pcols, LANES), lambda b, s, pt, ii=ii: (layer, pt[b, s * pps + ii], 0, 0))

    grid_spec = pltpu.PrefetchScalarGridSpec(
        num_scalar_prefetch=1,
        grid=(bsz, n_pages // pps),
        in_specs=[tok(0), tok(1), tok(2)] + [page_spec(ii) for ii in range(pps)] * 2
        + [pl.BlockSpec(lam_p.shape, lambda b, s, pt: (0, 0)), pl.BlockSpec((1, LANES), lambda b, s, pt: (0, 0))],
        out_specs=pl.BlockSpec((1, ts, d), lambda b, s, pt: (b, 0, 0)),
        scratch_shapes=[pltpu.VMEM((n_rows, LANES), BF16), pltpu.VMEM((n_rows, pcols), F32),
                        pltpu.VMEM((n_rows, 1), F32), pltpu.VMEM((n_rows, 1), F32), pltpu.VMEM((n_rows, LANES), F32)])
    return pl.pallas_call(
        functools.partial(_attn_sample_kernel, n_heads=n_heads, ts=ts, pages_per_step=pps, page=page,
                          past=n_pages * page, lam_init=lam_init),
        out_shape=jax.ShapeDtypeStruct((bsz, ts, d), F32),
        grid_spec=grid_spec,
        compiler_params=_params("parallel", "arbitrary"),
        name="attn_sample",
    )(page_table, proj, proj, proj, *([cache_k] * pps), *([cache_v] * pps), lam_p, subln_w.reshape(1, LANES))


def _ssd_kernel(*refs, hpg, headdim, conv_k, has_state):
    (x_ref, b_ref, c_ref, z_ref, dt_ref, wx_ref, wb_ref, wc_ref, bx_ref, bb_ref, bc_ref,
     dtb_ref, alog_ref, dvec_ref, nw_ref) = refs[:15]
    if has_state:
        cx_ref, cb_ref, cc_ref, s0_ref = refs[15:19]
        refs = refs[19:]
    else:
        refs = refs[15:]
    o_ref, so_ref, st_ref, buf_ref, cumt_ref, dtt_ref = refs
    g = pl.program_id(1)
    t = pl.program_id(2)
    tb = x_ref.shape[1]
    c = SSD_CHUNK
    gps = st_ref.shape[0]
    xw = x_ref.shape[2]
    bw = b_ref.shape[2]
    gw = xw // gps
    n_state = bw // gps
    assert headdim & (headdim - 1) == 0 and n_state == LANES and tb <= c
    pad = SUBLANES

    @pl.when(t == 0)
    def _():
        if has_state:
            st_ref[...] = s0_ref[0]
            buf_ref[0:pad, 0:xw] = cx_ref[0]
            buf_ref[0:pad, xw:xw + bw] = cb_ref[0]
            buf_ref[0:pad, xw + bw:] = cc_ref[0]
        else:
            st_ref[...] = jnp.zeros_like(st_ref)
            buf_ref[0:pad, :] = jnp.zeros((pad, buf_ref.shape[1]), F32)

    buf_ref[pad:pad + c, 0:xw] = _pad_rows(x_ref[0], c)
    buf_ref[pad:pad + c, xw:xw + bw] = _pad_rows(b_ref[0], c)
    buf_ref[pad:pad + c, xw + bw:] = _pad_rows(c_ref[0], c)
    w_all = jnp.concatenate([wx_ref[...], wb_ref[...], wc_ref[...]], axis=1)
    conv = jnp.concatenate([bx_ref[...], bb_ref[...], bc_ref[...]], axis=1)
    for j in range(conv_k):
        r0 = pad - (conv_k - 1) + j
        conv = conv + w_all[j:j + 1] * buf_ref[r0:r0 + c, :]
    tail = buf_ref[c:c + pad, :]
    buf_ref[0:pad, :] = tail
    act = _silu(conv)

    x_dt = dt_ref[0] + dtb_ref[...]
    dt = jnp.maximum(x_dt, 0.0) + jnp.log1p(jnp.exp(-jnp.abs(x_dt)))
    dt = _pad_rows(dt, c)
    if tb < c:
        dt = jnp.where(lax.broadcasted_iota(jnp.int32, dt.shape, 0) < tb, dt, 0.0)
    a = -jnp.exp(alog_ref[...])
    cum = _cumsum_rows(dt * a)
    cumt_ref[...] = cum.T
    dtt_ref[...] = dt.T
    head0 = g * (gps * hpg)
    shift = lax.rem(LANES - head0, LANES)
    cum_g = pltpu.roll(cum, shift, 1)
    dt_g = pltpu.roll(dt, shift, 1)

    tri = _tril(c)
    log_p = int(math.log2(headdim))
    lane_head = lax.shift_right_logical(lax.broadcasted_iota(jnp.int32, (c, gw), 1), log_p)
    row_head = lax.shift_right_logical(lax.broadcasted_iota(jnp.int32, (gw, n_state), 0), log_p)
    dvec = dvec_ref[...]
    nw = nw_ref[...]
    z = z_ref[0]
    for gi in range(gps):
        xs = act[:, gi * gw:(gi + 1) * gw]
        b_c = act[:, xw + gi * n_state:xw + (gi + 1) * n_state].astype(BF16)
        c_c = act[:, xw + bw + gi * n_state:xw + bw + (gi + 1) * n_state].astype(BF16)
        cb = lax.dot_general(c_c, b_c, _TRANS_B, preferred_element_type=F32)
        ms = []
        ecum = jnp.zeros((c, gw), F32)
        wsel = jnp.zeros((c, gw), F32)
        dec = jnp.zeros((gw, n_state), F32)
        for r in range(hpg):
            hl = gi * hpg + r
            cc = cum_g[:, hl:hl + 1]
            crow = cumt_ref[pl.ds(head0 + hl, 1), :]
            drow = dtt_ref[pl.ds(head0 + hl, 1), :]
            ms.append(jnp.where(tri, cb * jnp.exp(cc - crow) * drow, 0.0).astype(BF16))
            last = cum_g[c - 1:c, hl:hl + 1]
            ecum = jnp.where(lane_head == r, jnp.exp(cc), ecum)
            wsel = jnp.where(lane_head == r, dt_g[:, hl:hl + 1] * jnp.exp(last - cc), wsel)
            dec = jnp.where(row_head == r, jnp.exp(last), dec)
        y_all = jnp.dot(jnp.concatenate(ms, axis=0), xs.astype(BF16), preferred_element_type=F32)
        y = jnp.zeros((c, gw), F32)
        for r in range(hpg):
            y = jnp.where(lane_head == r, y_all[r * c:(r + 1) * c], y)
        sp = st_ref[gi]
        y = y + lax.dot_general(c_c, sp.astype(BF16), _TRANS_B, preferred_element_type=F32) * ecum
        upd = lax.dot_general((xs * wsel).astype(BF16), b_c, _TRANS_A, preferred_element_type=F32)
        st_ref[gi] = dec * sp + upd
        cols = slice(gi * gw, (gi + 1) * gw)
        y = y + dvec[:, cols] * xs
        y = y[:tb] * _silu(z[:, cols])
        o_ref[0, :, cols] = _rms(y, nw[:, cols])

    @pl.when(t == pl.num_programs(2) - 1)
    def _():
        so_ref[0] = st_ref[...]


def _ssd(proj, dtp, conv_w, conv_b, dt_bias, a_log, d_vec, norm_w, conv_state, s0, *, z_col, xbc_col, groups, hpg,
         headdim, n_state):
    bsz, t, _ = proj.shape
    gw = hpg * headdim
    d_inner = groups * gw
    conv_k = conv_w.shape[0]
    tb = SSD_CHUNK if t >= SSD_CHUNK else t
    gps = SSD_GROUPS_PER_STEP if groups % SSD_GROUPS_PER_STEP == 0 else 1
    xw, bw = gps * gw, gps * n_state
    assert t % tb == 0 and gw % LANES == 0 and n_state == LANES
    assert z_col % xw == 0 and xbc_col % xw == 0 and (xbc_col + d_inner) % bw == 0 and d_inner % bw == 0
    xb, bb = xbc_col // xw, (xbc_col + d_inner) // bw
    cb_ = bb + groups // gps
    zb = z_col // xw
    wb0 = d_inner // bw
    wc0 = wb0 + groups // gps

    def tok(width, blk0):
        return pl.BlockSpec((1, tb, width), lambda b, g, ti: (b, ti, blk0 + g))

    def par(rows, width, blk0):
        return pl.BlockSpec((rows, width), lambda b, g, ti: (0, blk0 + g))

    def full(shape):
        return pl.BlockSpec(shape, lambda b, g, ti: (0,) * len(shape))

    in_specs = [tok(xw, xb), tok(bw, bb), tok(bw, cb_), tok(xw, zb),
                pl.BlockSpec((1, tb, LANES), lambda b, g, ti: (b, ti, 0)),
                par(conv_k, xw, 0), par(conv_k, bw, wb0), par(conv_k, bw, wc0),
                par(1, xw, 0), par(1, bw, wb0), par(1, bw, wc0),
                full((1, LANES)), full((1, LANES)), par(1, xw, 0), par(1, xw, 0)]
    args = [proj, proj, proj, proj, dtp, conv_w, conv_w, conv_w, conv_b, conv_b, conv_b,
            dt_bias, a_log, d_vec, norm_w]
    if s0 is not None:
        def st(width, blk0):
            return pl.BlockSpec((1, SUBLANES, width), lambda b, g, ti: (b, 0, blk0 + g))
        in_specs += [st(xw, 0), st(bw, wb0), st(bw, wc0),
                     pl.BlockSpec((1, gps, gw, n_state), lambda b, g, ti: (b, g, 0, 0))]
        args += [conv_state, conv_state, conv_state, s0]
    return pl.pallas_call(
        functools.partial(_ssd_kernel, hpg=hpg, headdim=headdim, conv_k=conv_k, has_state=s0 is not None),
        out_shape=(jax.ShapeDtypeStruct((bsz, t, d_inner), F32),
                   jax.ShapeDtypeStruct((bsz, groups, gw, n_state), F32)),
        grid=(bsz, groups // gps, t // tb),
        in_specs=in_specs,
        out_specs=(pl.BlockSpec((1, tb, xw), lambda b, g, ti: (b, ti, g)),
                   pl.BlockSpec((1, gps, gw, n_state), lambda b, g, ti: (b, g, 0, 0))),
        scratch_shapes=[pltpu.VMEM((gps, gw, n_state), F32),
                        pltpu.VMEM((SSD_CHUNK + SUBLANES, xw + 2 * bw), F32),
                        pltpu.VMEM((LANES, SSD_CHUNK), F32), pltpu.VMEM((LANES, SSD_CHUNK), F32)],
        compiler_params=_params("parallel", "parallel", "arbitrary"),
        name="ssd",
    )(*args)


def _merge_kernel(x_ref, oa_ref, ob_ref, yc_ref, g0_ref, g1_ref, g2_ref, wa_ref, wb_ref, wc_ref, wo_ref, o_ref):
    def branch(gate_ref, v_ref, w_ref):
        return jax.nn.sigmoid(gate_ref[...]) * jnp.dot(v_ref[...].astype(BF16), w_ref[...], preferred_element_type=F32)

    merged = branch(g0_ref, oa_ref, wa_ref) + branch(g1_ref, ob_ref, wb_ref) + branch(g2_ref, yc_ref, wc_ref)
    o_ref[...] = x_ref[...] + jnp.dot(merged.astype(BF16), wo_ref[...], preferred_element_type=F32)


def _merge(x, o_a, o_b, y_c, proj, w_a, w_b, w_c, w_o, *, gate_col):
    n, d = x.shape
    tm = _pick_tile(n, (256, 128, 64, 32, 16, 8))
    gb = gate_col // d

    def tok(width, blk=0):
        return pl.BlockSpec((tm, width), lambda i, blk=blk: (i, blk))

    def weight(w):
        return pl.BlockSpec(w.shape, lambda i: (0, 0), pipeline_mode=pl.Buffered(1))

    return pl.pallas_call(
        _merge_kernel,
        out_shape=jax.ShapeDtypeStruct((n, d), F32),
        grid=(n // tm,),
        in_specs=[tok(d), tok(d), tok(d), tok(y_c.shape[1]), tok(d, gb), tok(d, gb + 1), tok(d, gb + 2),
                  weight(w_a), weight(w_b), weight(w_c), weight(w_o)],
        out_specs=tok(d),
        compiler_params=_params("parallel"),
        name="merge",
    )(x, o_a, o_b, y_c, proj, proj, proj, w_a, w_b, w_c, w_o)


def _xattn_kernel(x_ref, mk_ref, mv_ref, g_ref, wq_ref, wo_ref, o_ref, *, n_heads):
    x = x_ref[0]
    d = x.shape[1]
    dh = d // n_heads
    q = jnp.dot(_rms(x, g_ref[...]).astype(BF16), wq_ref[...], preferred_element_type=F32) * (dh ** -0.5)
    outs = []
    for h in range(n_heads):
        cols = slice(h * dh, (h + 1) * dh)
        k_b = mk_ref[0, :, cols].astype(BF16)
        v_b = mv_ref[0, :, cols].astype(BF16)
        s = lax.dot_general(q[:, cols].astype(BF16), k_b, _TRANS_B, preferred_element_type=F32)
        p = jnp.exp(s - jnp.max(s, axis=-1, keepdims=True))
        o_h = jnp.dot(p.astype(BF16), v_b, preferred_element_type=F32)
        outs.append(o_h / jnp.sum(p, axis=-1, keepdims=True))
    o_m = jnp.concatenate(outs, axis=1).astype(BF16)
    o_ref[0] = x + jnp.dot(o_m, wo_ref[...], preferred_element_type=F32)


def _xattn(x, mem_k, mem_v, g, w_q, w_o, *, n_heads):
    bsz, t, d = x.shape
    m = mem_k.shape[1]
    tq = _pick_tile(t, (512, 256, 128, 64, 32, 16, 8))

    def weight(w):
        return pl.BlockSpec(w.shape, lambda b, ti: (0, 0), pipeline_mode=pl.Buffered(1))

    return pl.pallas_call(
        functools.partial(_xattn_kernel, n_heads=n_heads),
        out_shape=jax.ShapeDtypeStruct((bsz, t, d), F32),
        grid=(bsz, t // tq),
        in_specs=[pl.BlockSpec((1, tq, d), lambda b, ti: (b, ti, 0)),
                  pl.BlockSpec((1, m, d), lambda b, ti: (b, 0, 0)),
                  pl.BlockSpec((1, m, d), lambda b, ti: (b, 0, 0)),
                  pl.BlockSpec((1, d), lambda b, ti: (0, 0)),
                  weight(w_q), weight(w_o)],
        out_specs=pl.BlockSpec((1, tq, d), lambda b, ti: (b, ti, 0)),
        compiler_params=_params("parallel", "parallel"),
        name="xattn",
    )(x, mem_k, mem_v, g.reshape(1, d), w_q, w_o)


def _ffn_kernel(*refs, conv_k, has_state, final_norm):
    x_ref, g_ref, wg_ref, wv_ref, wd_ref, cw_ref, cb_ref = refs[:7]
    refs = refs[7:]
    if has_state:
        cs_ref, refs = refs[0], refs[1:]
    if final_norm:
        fn_ref, refs = refs[0], refs[1:]
    o_ref, fo_ref, hn_ref, acc_ref, buf_ref, tail_ref = refs
    t = pl.program_id(1)
    f = pl.program_id(2)
    tt = x_ref.shape[1]
    pad = SUBLANES

    @pl.when(f == 0)
    def _():
        hn_ref[...] = _rms(x_ref[0], g_ref[...]).astype(BF16)
        acc_ref[...] = jnp.zeros_like(acc_ref)

    @pl.when(t == 0)
    def _():
        if has_state:
            tail_ref[f] = cs_ref[0]
        else:
            tail_ref[f] = jnp.zeros(tail_ref.shape[1:], F32)

    hn = hn_ref[...]
    gate = jnp.dot(hn, wg_ref[...], preferred_element_type=F32)
    val = jnp.dot(hn, wv_ref[...], preferred_element_type=F32)
    buf_ref[0:pad, :] = tail_ref[f]
    buf_ref[pad:pad + tt, :] = gate
    cw = cw_ref[...]
    conv = cb_ref[...]
    for j in range(conv_k):
        r0 = pad - (conv_k - 1) + j
        conv = conv + cw[j:j + 1] * buf_ref[r0:r0 + tt, :]
    tail_ref[f] = buf_ref[tt:tt + pad, :]
    fo_ref[0, 0] = buf_ref[pad + tt - (conv_k - 1):pad + tt, :]
    acc_ref[...] += jnp.dot((_silu(conv) * val).astype(BF16), wd_ref[...], preferred_element_type=F32)

    @pl.when(f == pl.num_programs(2) - 1)
    def _():
        y = x_ref[0] + acc_ref[...]
        if final_norm:
            y = _rms(y, fn_ref[...])
        o_ref[0] = y


def _ffn(x, g, w_up, w_down, conv_w, conv_b, conv_state, final_w):
    bsz, t, d = x.shape
    ff = w_down.shape[0]
    conv_k = conv_w.shape[0]
    tt = _pick_tile(t, (512, 256, 128, 64, 32, 16, 8))
    tf = _pick_tile(ff, (1408, 1024, 512, 256, 128))
    nf = ff // tf
    assert tt >= conv_k - 1
    in_specs = [pl.BlockSpec((1, tt, d), lambda b, ti, f: (b, ti, 0)),
                pl.BlockSpec((1, d), lambda b, ti, f: (0, 0)),
                pl.BlockSpec((d, tf), lambda b, ti, f: (0, f)),
                pl.BlockSpec((d, tf), lambda b, ti, f: (0, nf + f)),
                pl.BlockSpec((tf, d), lambda b, ti, f: (f, 0)),
                pl.BlockSpec((conv_k, tf), lambda b, ti, f: (0, f)),
                pl.BlockSpec((1, tf), lambda b, ti, f: (0, f))]
    args = [x, g.reshape(1, d), w_up, w_up, w_down, conv_w, conv_b.reshape(1, ff)]
    if conv_state is not None:
        in_specs.append(pl.BlockSpec((1, SUBLANES, tf), lambda b, ti, f: (b, 0, f)))
        args.append(conv_state)
    if final_w is not None:
        in_specs.append(pl.BlockSpec((1, d), lambda b, ti, f: (0, 0)))
        args.append(final_w.reshape(1, d))
    y, tails = pl.pallas_call(
        functools.partial(_ffn_kernel, conv_k=conv_k, has_state=conv_state is not None,
                          final_norm=final_w is not None),
        out_shape=(jax.ShapeDtypeStruct((bsz, t, d), F32),
                   jax.ShapeDtypeStruct((bsz, t // tt, conv_k - 1, ff), F32)),
        grid=(bsz, t // tt, nf),
        in_specs=in_specs,
        out_specs=(pl.BlockSpec((1, tt, d), lambda b, ti, f: (b, ti, 0)),
                   pl.BlockSpec((1, 1, conv_k - 1, tf), lambda b, ti, f: (b, ti, 0, f))),
        scratch_shapes=[pltpu.VMEM((tt, d), BF16), pltpu.VMEM((tt, d), F32),
                        pltpu.VMEM((tt + SUBLANES, tf), F32), pltpu.VMEM((nf, SUBLANES, tf), F32)],
        compiler_params=_params("parallel", "arbitrary", "arbitrary"),
        name="ffn",
    )(*args)
    return y, tails[:, -1]


def _pad_state_rows(s):
    return jnp.pad(s, ((0, 0), (SUBLANES - s.shape[1], 0), (0, 0)))


def _pad_lanes(v):
    return jnp.pad(v, (0, LANES - v.shape[0])).reshape(1, LANES)


def kernel(x_prompt, x_sample, cache_k, cache_v, cache_mem_k, cache_mem_v, state_hgrn, state_ssm, state_conv, state_ffn_conv, page_table, mem_prompt, norm_mix, w_in, hgrn_lb_logits, hgrn_out_norm, diff_lambda, diff_subln, ssm_conv_w, ssm_conv_b, ssm_dt_bias, ssm_a_log, ssm_d, ssm_norm, w_br_a, w_br_b, w_br_c, w_out, norm_mem_q, norm_mem_kv, w_mq, w_mk, w_mv, w_mo, norm_ffn, w_up, ffn_conv_w, ffn_conv_b, w_down, norm_final):
    depth = w_in.shape[0]
    bp, tp, d = x_prompt.shape
    bs, ts, _ = x_sample.shape
    n_heads = d // LANES
    groups, hpg, headdim, n_state = state_ssm.shape[2:]
    d_inner = groups * hpg * headdim
    conv_dim = state_conv.shape[-1]
    ssm_heads = groups * hpg
    mem_len, mem_heads = cache_mem_k.shape[2], cache_mem_k.shape[3]
    n_phys, page = cache_k.shape[1], cache_k.shape[2]
    assert ssm_heads <= LANES and conv_dim == d_inner + 2 * groups * n_state
    z_col = 7 * d
    xbc_col = z_col + d_inner
    dt_col = xbc_col + conv_dim
    gate_col = dt_col

    slopes = jnp.exp2(-8.0 * jnp.arange(1, n_heads + 1, dtype=F32) / n_heads)
    cache_k2 = cache_k.reshape(depth, n_phys, page * n_heads, LANES)
    cache_v2 = cache_v.reshape(depth, n_phys, page * n_heads, LANES)

    xp, xs = x_prompt, x_sample
    outs = [[] for _ in range(14)]
    for l in range(depth):
        lam_init = 0.8 - 0.6 * math.exp(-0.3 * l)
        last = l == depth - 1
        w_main = jnp.concatenate([w_in[l][:, :dt_col], w_in[l][:, dt_col + ssm_heads:]], axis=1).astype(BF16)
        w_dt = jnp.pad(w_in[l][:, dt_col:dt_col + ssm_heads], ((0, 0), (0, LANES - ssm_heads))).astype(BF16)
        wa, wb, wc, wo = (w[l].astype(BF16) for w in (w_br_a, w_br_b, w_br_c, w_out))
        wq, wk, wv, wmo = (w[l].astype(BF16) for w in (w_mq, w_mk, w_mv, w_mo))
        wup, wdn = w_up[l].astype(BF16), w_down[l].astype(BF16)
        dt_bias, a_log = _pad_lanes(ssm_dt_bias[l]), _pad_lanes(ssm_a_log[l])
        d_vec = jnp.repeat(ssm_d[l], headdim).reshape(1, d_inner)
        ssm_nw = ssm_norm[l].reshape(1, d_inner)
        conv_b = ssm_conv_b[l].reshape(1, conv_dim)

        mem_n = mem_prompt.reshape(bp * mem_len, d)
        mk = _norm_matmul(mem_n, norm_mem_kv[l], wk).reshape(bp, mem_len, d)
        mv = _norm_matmul(mem_n, norm_mem_kv[l], wv).reshape(bp, mem_len, d)

        def trunk(x, mem_k, mem_v, hgrn_s0, ssm_s0, conv_s, ffn_s, attend):
            bsz, t, _ = x.shape
            x2 = x.reshape(bsz * t, d)
            proj2 = _norm_matmul(x2, norm_mix[l], w_main)
            proj = proj2.reshape(bsz, t, -1)
            dtp = _norm_matmul(x2, norm_mix[l], w_dt).reshape(bsz, t, LANES)
            o_a, s_a = _hgrn(proj, hgrn_lb_logits, hgrn_out_norm[l], hgrn_s0, layer=l, n_heads=n_heads, col0=0)
            o_b = attend(proj)
            y_c, s_c = _ssd(proj, dtp, ssm_conv_w[l], conv_b, dt_bias, a_log, d_vec, ssm_nw, conv_s,
                            None if ssm_s0 is None else ssm_s0.reshape(bsz, groups, hpg * headdim, n_state),
                            z_col=z_col, xbc_col=xbc_col, groups=groups, hpg=hpg, headdim=headdim, n_state=n_state)
            x2 = _merge(x2, o_a.reshape(bsz * t, d), o_b.reshape(bsz * t, d), y_c.reshape(bsz * t, d_inner), proj2,
                        wa, wb, wc, wo, gate_col=gate_col)
            x3 = _xattn(x2.reshape(bsz, t, d), mem_k, mem_v, norm_mem_q[l], wq, wmo, n_heads=mem_heads)
            x3, ffn_new = _ffn(x3, norm_ffn[l], wup, wdn, ffn_conv_w[l], ffn_conv_b[l], ffn_s,
                               norm_final if last else None)
            k_new = proj[:, :, 5 * d:6 * d].reshape(bsz, t, n_heads, LANES)
            v_new = proj[:, :, 6 * d:7 * d].reshape(bsz, t, n_heads, LANES)
            conv_new = proj[:, t - (ssm_conv_w.shape[1] - 1):, xbc_col:xbc_col + conv_dim]
            return x3, (k_new, v_new, s_a, s_c.reshape(bsz, groups, hpg, headdim, n_state), conv_new, ffn_new)

        xp, st_p = trunk(
            xp, mk, mv, None, None, None, None,
            lambda proj: _attn_prompt(proj, slopes, diff_lambda[l], diff_subln[l], n_heads=n_heads,
                                      col0=4 * n_heads, lam_init=lam_init))
        xs, st_s = trunk(
            xs, cache_mem_k[l].reshape(bs, mem_len, d), cache_mem_v[l].reshape(bs, mem_len, d),
            state_hgrn[l], state_ssm[l], _pad_state_rows(state_conv[l]), _pad_state_rows(state_ffn_conv[l]),
            lambda proj: _attn_sample(proj, cache_k2, cache_v2, page_table, diff_lambda[l], diff_subln[l],
                                      layer=l, n_heads=n_heads, col0=4, lam_init=lam_init))
        new = (st_p[0], st_p[1], st_s[0], st_s[1],
               mk.reshape(bp, mem_len, mem_heads, d // mem_heads), mv.reshape(bp, mem_len, mem_heads, d // mem_heads),
               st_p[2], st_s[2], st_p[3], st_s[3], st_p[4], st_s[4], st_p[5], st_s[5])
        for acc, val in zip(outs, new):
            acc.append(val)
    return (xp, xs) + tuple(jnp.stack(o) for o in outs)
```

```python
import functools
import math

import jax
import jax.numpy as jnp
from jax import lax
from jax.experimental import pallas as pl
from jax.experimental.pallas import tpu as pltpu

F32 = jnp.float32
BF16 = jnp.bfloat16
EPS = 1e-6
MASK_VALUE = -1e30
LANES = 128
SUBLANES = 8
HGRN_CHUNK = 64
HGRN_HEADS_PER_STEP = 2
SSD_GROUPS_PER_STEP = 4
PAGES_PER_STEP = 8
SSD_CHUNK = 128
VMEM_LIMIT = 48 * 1024 * 1024

_TRANS_B = (((1,), (1,)), ((), ()))
_TRANS_A = (((0,), (0,)), ((), ()))


def _params(*sem):
    return pltpu.CompilerParams(dimension_semantics=sem, vmem_limit_bytes=VMEM_LIMIT)


def _silu(x):
    return x * jax.nn.sigmoid(x)


def _rms(x, w):
    return x * lax.rsqrt(jnp.mean(x * x, axis=-1, keepdims=True) + EPS) * w


def _pad_rows(x, rows):
    if x.shape[0] == rows:
        return x
    return jnp.concatenate([x, jnp.zeros((rows - x.shape[0],) + x.shape[1:], x.dtype)], axis=0)


def _tril(n):
    r = lax.broadcasted_iota(jnp.int32, (n, n), 0)
    c = lax.broadcasted_iota(jnp.int32, (n, n), 1)
    return r >= c


def _select_sum_rows(w01, x):
    n = x.shape[1]
    hi = x.astype(BF16)
    r1 = x - hi.astype(F32)
    mid = r1.astype(BF16)
    lo = (r1 - mid.astype(F32)).astype(BF16)
    y = jnp.dot(w01.astype(BF16), jnp.concatenate([hi, mid, lo], axis=1), preferred_element_type=F32)
    return y[:, :n] + y[:, n:2 * n] + y[:, 2 * n:]


def _cumsum_rows(x):
    return _select_sum_rows(_tril(x.shape[0]).astype(F32), x)


def _pick_tile(n, candidates):
    for c in candidates:
        if n % c == 0:
            return c
    return n


def _norm_matmul_kernel(x_ref, g_ref, w_ref, o_ref, xn_ref):
    @pl.when(pl.program_id(1) == 0)
    def _():
        xn_ref[...] = _rms(x_ref[...], g_ref[...]).astype(BF16)

    o_ref[...] = jnp.dot(xn_ref[...], w_ref[...], preferred_element_type=F32)


def _norm_matmul(x, g, w):
    n, d = x.shape
    m = w.shape[1]
    tm = _pick_tile(n, (1024, 512, 256, 128, 64, 32, 16, 8))
    tn = _pick_tile(m, (1024, 512, 256, 128))
    return pl.pallas_call(
        _norm_matmul_kernel,
        out_shape=jax.ShapeDtypeStruct((n, m), F32),
        grid=(n // tm, m // tn),
        in_specs=[pl.BlockSpec((tm, d), lambda i, j: (i, 0)),
                  pl.BlockSpec((1, d), lambda i, j: (0, 0)),
                  pl.BlockSpec((d, tn), lambda i, j: (0, j))],
        out_specs=pl.BlockSpec((tm, tn), lambda i, j: (i, j)),
        scratch_shapes=[pltpu.VMEM((tm, d), BF16)],
        compiler_params=_params("parallel", "arbitrary"),
        name="norm_matmul",
    )(x, g.reshape(1, d), w)


def _norm_matmul_layer_kernel(x_ref, g_ref, w_ref, *rest):
    _norm_matmul_kernel(x_ref, g_ref, w_ref, *rest[-2:])


def _norm_matmul_into(x, g, w, stack, layer, depth):
    n, d = x.shape
    m = w.shape[1]
    tm = _pick_tile(n, (1024, 512, 256, 128, 64, 32, 16, 8))
    tn = _pick_tile(m, (1024, 512, 256, 128))
    in_specs = [pl.BlockSpec((tm, d), lambda i, j: (i, 0)),
                pl.BlockSpec((1, d), lambda i, j: (0, 0)),
                pl.BlockSpec((d, tn), lambda i, j: (0, j))]
    args = [x, g.reshape(1, d), w]
    if stack is not None:
        in_specs.append(pl.BlockSpec(memory_space=pl.ANY))
        args.append(stack)
    return pl.pallas_call(
        _norm_matmul_layer_kernel,
        out_shape=jax.ShapeDtypeStruct((depth, n, m), F32),
        grid=(n // tm, m // tn),
        in_specs=in_specs,
        out_specs=pl.BlockSpec((None, tm, tn), lambda i, j: (layer, i, j)),
        scratch_shapes=[pltpu.VMEM((tm, d), BF16)],
        input_output_aliases={} if stack is None else {3: 0},
        compiler_params=_params("parallel", "arbitrary"),
        name="norm_matmul_into",
    )(*args)


def _cumsum_and_midpoints(x, levels):
    c = x.shape[0]
    sub = jnp.bitwise_and(lax.broadcasted_iota(jnp.int32, x.shape, 0), SUBLANES - 1)
    for d in (1, 2, 4):
        x = x + jnp.where(sub >= d, pltpu.roll(x, d, 0), 0.0)
    groups, total = [], None
    for v in range(c // SUBLANES):
        grp = x[v * SUBLANES:(v + 1) * SUBLANES]
        if total is not None:
            grp = grp + total
        total = grp[SUBLANES - 1:SUBLANES]
        groups.append(grp)
    b = jnp.concatenate(groups, axis=0)
    mids = []
    for m in levels:
        if m == 1:
            mids.append(jnp.where(jnp.bitwise_and(sub, 1) == 1, pltpu.roll(b, 1, 0), b))
        elif m == 2:
            s4 = jnp.bitwise_and(sub, 3)
            mids.append(jnp.where(s4 == 0, pltpu.roll(b, c - 1, 0),
                                  jnp.where(s4 == 1, b, jnp.where(s4 == 2, pltpu.roll(b, 1, 0), pltpu.roll(b, 2, 0)))))
        else:
            mids.append(jnp.concatenate(
                [jnp.broadcast_to(b[blk * 2 * m + m - 1:blk * 2 * m + m], (2 * m, x.shape[1]))
                 for blk in range(c // (2 * m))], axis=0))
    return b, mids


def _hgrn_kernel(*refs, layer, has_s0):
    if has_s0:
        q_ref, f_ref, i_ref, g_ref, lb_ref, nw_ref, s0_ref, o_ref, so_ref, st_ref = refs
    else:
        q_ref, f_ref, i_ref, g_ref, lb_ref, nw_ref, o_ref, so_ref, st_ref = refs
    t = pl.program_id(2)
    tb = q_ref.shape[1]
    c = HGRN_CHUNK
    n_chunks = max(1, tb // c)
    rows = min(tb, c)
    hps = st_ref.shape[0]

    @pl.when(t == 0)
    def _():
        for hh in range(hps):
            if has_s0:
                st_ref[hh] = s0_ref[0, hh].T
            else:
                st_ref[hh] = jnp.zeros((LANES, LANES), F32)

    lbl = lb_ref[...]
    e = jnp.exp(lbl - jnp.max(lbl, axis=0, keepdims=True))
    soft = e / jnp.sum(e, axis=0, keepdims=True)
    lb = jnp.sum(soft[:layer + 1], axis=0, keepdims=True) - soft[0:1]

    nw = nw_ref[...]
    row_id = lax.broadcasted_iota(jnp.int32, (c, LANES), 0)
    assert 2 * c == LANES
    levels = [1 << j for j in range(int(math.log2(c)))]
    upper = [jnp.bitwise_and(row_id, m) != 0 for m in levels]
    group = 4
    slots = [None] + levels
    slots += [0] * (-len(slots) % group)
    lane_id = lax.broadcasted_iota(jnp.int32, (c, LANES), 1)
    key = jnp.bitwise_and(lane_id, c - 1)
    lane_half = lax.shift_right_logical(lane_id, int(math.log2(c)))
    slot_mask = []
    for si, m in enumerate(slots):
        in_half = lane_half == (si % 2)
        if m is None:
            slot_mask.append(jnp.logical_and(in_half, row_id == key))
        elif m > 0:
            slot_mask.append(jnp.logical_and(in_half, jnp.bitwise_and(row_id, -2 * m) == jnp.bitwise_and(key, -2 * m)))
        else:
            slot_mask.append(None)

    chunks = []
    for hh in range(hps):
        hl = slice(hh * LANES, (hh + 1) * LANES)
        for ci in range(n_chunks):
            sl = slice(ci * c, ci * c + rows)
            f = _pad_rows(f_ref[0, sl, hl], c)
            q = _silu(_pad_rows(q_ref[0, sl, hl], c))
            v = _pad_rows(i_ref[0, sl, hl], c)
            forget = lb[:, hl] + (1.0 - lb[:, hl]) * jax.nn.sigmoid(f)
            logf = jnp.log(forget)
            kk = 1.0 - forget
            if rows < c:
                logf = jnp.where(row_id < rows, logf, 0.0)
                kk = jnp.where(row_id < rows, kk, 0.0)
            chunks.append((hh, hl, sl, q, v, kk, logf))
    states = [st_ref[hh] for hh in range(hps)]
    zero_slot = jnp.zeros((c, LANES), BF16)
    for idx, (hh, hl, sl, q, v, kk, logf) in enumerate(chunks):
        st = states[hh]
        b, brefs = _cumsum_and_midpoints(logf, levels)
        v_b = v.astype(BF16)
        q_parts, k_parts = [q.astype(BF16)], [kk.astype(BF16)]
        for li in range(len(levels)):
            bref = brefs[li]
            q_parts.append(jnp.where(upper[li], q * jnp.exp(b - bref), 0.0).astype(BF16))
            k_parts.append(jnp.where(upper[li], 0.0, kk * jnp.exp(bref - b)).astype(BF16))
        q_parts += [zero_slot] * (len(slots) - len(q_parts))
        k_parts += [zero_slot] * (len(slots) - len(k_parts))
        scores = jnp.zeros((c, LANES), F32)
        for g0 in range(0, len(slots), group):
            s_g = lax.dot_general(jnp.concatenate(q_parts[g0:g0 + group], axis=0),
                                  jnp.concatenate(k_parts[g0:g0 + group], axis=0), _TRANS_B,
                                  preferred_element_type=F32)
            for i in range(group):
                if slot_mask[g0 + i] is not None:
                    tile = s_g[i * c:(i + 1) * c, (i // 2) * LANES:(i // 2 + 1) * LANES]
                    scores = scores + jnp.where(slot_mask[g0 + i], tile, 0.0)
        o = lax.dot_general((q * jnp.exp(b)).astype(BF16), st.astype(BF16), _TRANS_B, preferred_element_type=F32)
        o = o + jnp.dot(scores.astype(BF16), jnp.concatenate([v_b, v_b], axis=0), preferred_element_type=F32)
        last = b[c - 1:c]
        kdec = (kk * jnp.exp(last - b)).astype(BF16)
        states[hh] = jnp.exp(last) * st + lax.dot_general(v_b, kdec, _TRANS_A, preferred_element_type=F32)
        g = g_ref[0, sl, hl]
        o_ref[0, sl, hl] = _rms(o[:rows], nw) * _silu(g)
    for hh in range(hps):
        st_ref[hh] = states[hh]

    @pl.when(t == pl.num_programs(2) - 1)
    def _():
        for hh in range(hps):
            so_ref[0, hh] = states[hh].T


def _hgrn(proj, lb_logits, norm_w, s0, *, layer, n_heads, col0):
    bsz, t, _ = proj.shape
    tb = _pick_tile(t, (512, 256, 128, 64)) if t >= HGRN_CHUNK else t
    depth = lb_logits.shape[0]
    hps = HGRN_HEADS_PER_STEP if n_heads % HGRN_HEADS_PER_STEP == 0 and col0 % HGRN_HEADS_PER_STEP == 0 else 1
    width = hps * LANES

    def col(k):
        return pl.BlockSpec((1, tb, width), lambda b, h, ti, k=k: (b, ti, (col0 + k * n_heads) // hps + h))

    in_specs = [col(0), col(1), col(2), col(3),
                pl.BlockSpec((depth, width), lambda b, h, ti: (0, h)),
                pl.BlockSpec((1, LANES), lambda b, h, ti: (0, 0))]
    args = [proj, proj, proj, proj, lb_logits, norm_w.reshape(1, LANES)]
    if s0 is not None:
        in_specs.append(pl.BlockSpec((1, hps, LANES, LANES), lambda b, h, ti: (b, h, 0, 0)))
        args.append(s0)
    return pl.pallas_call(
        functools.partial(_hgrn_kernel, layer=layer, has_s0=s0 is not None),
        out_shape=(jax.ShapeDtypeStruct((bsz, t, n_heads * LANES), F32),
                   jax.ShapeDtypeStruct((bsz, n_heads, LANES, LANES), F32)),
        grid=(bsz, n_heads // hps, t // tb),
        in_specs=in_specs,
        out_specs=(pl.BlockSpec((1, tb, width), lambda b, h, ti: (b, ti, h)),
                   pl.BlockSpec((1, hps, LANES, LANES), lambda b, h, ti: (b, h, 0, 0))),
        scratch_shapes=[pltpu.VMEM((hps, LANES, LANES), F32)],
        compiler_params=_params("parallel", "parallel", "arbitrary"),
        name="hgrn",
    )(*args)


def _diff_lambda(lam_ref, lam_init):
    l4 = lam_ref[...]
    return (jnp.exp(jnp.sum(l4[0:1] * l4[1:2], axis=-1, keepdims=True))
            - jnp.exp(jnp.sum(l4[2:3] * l4[3:4], axis=-1, keepdims=True)) + lam_init)


def _softmax_update(s, v_b, m_ref, l_ref, acc_ref):
    m_old = m_ref[...]
    m_new = jnp.maximum(m_old, jnp.max(s, axis=-1, keepdims=True))
    alpha = jnp.exp(m_old - m_new)
    p = jnp.exp(s - m_new)
    l_ref[...] = alpha * l_ref[...] + jnp.sum(p, axis=-1, keepdims=True)
    acc_ref[...] = alpha * acc_ref[...] + jnp.dot(p.astype(BF16), v_b, preferred_element_type=F32)
    m_ref[...] = m_new


def _attn_prompt_kernel(slope_ref, q_ref, k_ref, v_ref, lam_ref, nw_ref, o_ref, kb_ref, vb_ref, bias_ref, bdiag_ref,
                        s_ref, m_ref, acc_ref, *, blk, lam_init):
    h = pl.program_id(1)
    qi = pl.program_id(2)
    half = LANES // 2
    n_sub = blk // LANES
    neg_slope = -slope_ref[h]

    @pl.when(qi == 0)
    def _():
        kb_ref[...] = k_ref[0].astype(BF16)
        vb_ref[:, :LANES] = v_ref[0].astype(BF16)
        vb_ref[:, LANES:] = jnp.ones((vb_ref.shape[0], LANES), BF16)
        r = lax.broadcasted_iota(jnp.int32, (2 * blk, blk), 0)
        r = jnp.where(r >= blk, r - blk, r)
        c = lax.broadcasted_iota(jnp.int32, (2 * blk, blk), 1)
        bias0 = neg_slope * (r - c).astype(F32)
        bias_ref[...] = bias0
        bdiag_ref[...] = jnp.where(c <= r, bias0, MASK_VALUE)

    q = q_ref[0] * (half ** -0.5)
    lane = lax.broadcasted_iota(jnp.int32, q.shape, 1)
    qs = jnp.concatenate([jnp.where(lane < half, q, 0.0), jnp.where(lane >= half, q, 0.0)], axis=0).astype(BF16)

    def scores(j, bias):
        start = pl.multiple_of(j * blk, blk)
        s = lax.dot_general(qs, kb_ref[pl.ds(start, blk), :], _TRANS_B, preferred_element_type=F32) + bias
        s_ref[j] = s
        m = m_ref[...]
        for u in range(n_sub):
            m = jnp.maximum(m, s[:, u * LANES:(u + 1) * LANES])
        m_ref[...] = m

    m_ref[...] = jnp.full_like(m_ref, MASK_VALUE)

    def pass1(j, carry):
        scores(j, bias_ref[...] + neg_slope * ((qi - j) * blk).astype(F32))
        return carry

    lax.fori_loop(0, qi, pass1, 0)
    scores(qi, bdiag_ref[...])
    m_row = jnp.broadcast_to(jnp.max(m_ref[...], axis=-1, keepdims=True), m_ref.shape)

    acc_ref[...] = jnp.zeros_like(acc_ref)

    def pass2(j, carry):
        start = pl.multiple_of(j * blk, blk)
        s = s_ref[j]
        ps = [jnp.exp(s[:, u * LANES:(u + 1) * LANES] - m_row).astype(BF16) for u in range(n_sub)]
        acc_ref[...] += jnp.dot(jnp.concatenate(ps, axis=1), vb_ref[pl.ds(start, blk), :], preferred_element_type=F32)
        return carry

    lax.fori_loop(0, qi + 1, pass2, 0)

    lam = _diff_lambda(lam_ref, lam_init)
    acc = acc_ref[:, :LANES]
    l = acc_ref[:, LANES:]
    o = acc[:blk] / l[:blk] - lam * (acc[blk:] / l[blk:])
    o_ref[0] = _rms(o, nw_ref[...]) * (1.0 - lam_init)


def _attn_prompt(proj, k_all, v_all, slopes, lam_p, subln_w, *, layer, n_heads, col0, lam_init):
    bsz, t, _ = proj.shape
    blk = _pick_tile(t, (512, 256, 128))
    return pl.pallas_call(
        functools.partial(_attn_prompt_kernel, blk=blk, lam_init=lam_init),
        out_shape=jax.ShapeDtypeStruct((bsz, t, n_heads * LANES), F32),
        grid=(bsz, n_heads, t // blk),
        in_specs=[pl.BlockSpec(memory_space=pltpu.SMEM),
                  pl.BlockSpec((1, blk, LANES), lambda b, h, qi: (b, qi, col0 + h)),
                  pl.BlockSpec((None, 1, t, LANES), lambda b, h, qi: (layer, b, 0, h)),
                  pl.BlockSpec((None, 1, t, LANES), lambda b, h, qi: (layer, b, 0, h)),
                  pl.BlockSpec(lam_p.shape, lambda b, h, qi: (0, 0)),
                  pl.BlockSpec((1, LANES), lambda b, h, qi: (0, 0))],
        out_specs=pl.BlockSpec((1, blk, LANES), lambda b, h, qi: (b, qi, h)),
        scratch_shapes=[pltpu.VMEM((t, LANES), BF16), pltpu.VMEM((t, 2 * LANES), BF16),
                        pltpu.VMEM((2 * blk, blk), F32), pltpu.VMEM((2 * blk, blk), F32),
                        pltpu.VMEM((t // blk, 2 * blk, blk), F32),
                        pltpu.VMEM((2 * blk, LANES), F32), pltpu.VMEM((2 * blk, 2 * LANES), F32)],
        compiler_params=_params("parallel", "parallel", "arbitrary"),
        name="attn_prompt",
    )(slopes, proj, k_all, v_all, lam_p, subln_w.reshape(1, LANES))


def _attn_sample_kernel(pt_ref, q_ref, kn_ref, vn_ref, *refs, n_heads, ts, pages_per_step, page, past, lam_init):
    k_refs = refs[:pages_per_step]
    v_refs = refs[pages_per_step:2 * pages_per_step]
    lam_ref, nw_ref, o_ref, q2_ref, base_ref, m_ref, l_ref, acc_ref = refs[2 * pages_per_step:]
    del pt_ref
    step = pl.program_id(1)
    half = LANES // 2
    n_rows = n_heads * 2 * ts
    pcols = page * n_heads
    log_ts = int(math.log2(ts))
    log_h = int(math.log2(n_heads))
    row1 = lax.broadcasted_iota(jnp.int32, (n_rows, 1), 0)
    head = lax.shift_right_logical(row1, log_ts + 1)
    slope = jnp.exp2(-8.0 * (head + 1).astype(F32) / n_heads)
    q_tok = jnp.bitwise_and(row1, ts - 1)

    @pl.when(step == 0)
    def _():
        q = q_ref[0] * (half ** -0.5)
        lane = lax.broadcasted_iota(jnp.int32, (ts, LANES), 1)
        parts = []
        for h in range(n_heads):
            q_h = q[:, h * LANES:(h + 1) * LANES]
            parts += [jnp.where(lane < half, q_h, 0.0), jnp.where(lane >= half, q_h, 0.0)]
        q2_ref[...] = jnp.concatenate(parts, axis=0).astype(BF16)
        col = lax.broadcasted_iota(jnp.int32, (n_rows, pcols), 1)
        col_tok = lax.shift_right_logical(col, log_h)
        base_ref[...] = jnp.where(jnp.bitwise_and(col, n_heads - 1) == head,
                                  -slope * (past + q_tok - col_tok).astype(F32), MASK_VALUE)
        m_ref[...] = jnp.full_like(m_ref, MASK_VALUE)
        l_ref[...] = jnp.zeros_like(l_ref)
        acc_ref[...] = jnp.zeros_like(acc_ref)
        k_new, v_new = kn_ref[0], vn_ref[0]
        k_self = jnp.concatenate([k_new[:, h * LANES:(h + 1) * LANES] for h in range(n_heads)], axis=0)
        v_self = jnp.concatenate([v_new[:, h * LANES:(h + 1) * LANES] for h in range(n_heads)], axis=0)
        n_self = max(LANES, n_heads * ts)
        k_self = _pad_rows(k_self, n_self).astype(BF16)
        v_self = _pad_rows(v_self, n_self).astype(BF16)
        c_self = lax.broadcasted_iota(jnp.int32, (n_rows, n_self), 1)
        c_head = lax.shift_right_logical(c_self, log_ts)
        c_tok = jnp.bitwise_and(c_self, ts - 1)
        s = lax.dot_general(q2_ref[...], k_self, _TRANS_B, preferred_element_type=F32)
        s = jnp.where(c_head == head, jnp.where(c_tok <= q_tok, s - slope * (q_tok - c_tok).astype(F32), MASK_VALUE),
                      MASK_VALUE)
        _softmax_update(s, v_self, m_ref, l_ref, acc_ref)

    q2 = q2_ref[...]
    m_old = m_ref[...]
    m_new = m_old
    s_parts = []
    for ii in range(pages_per_step):
        k_b = k_refs[ii][0, 0].astype(BF16)
        s = lax.dot_general(q2, k_b, _TRANS_B, preferred_element_type=F32) + base_ref[...]
        off = slope * ((step * pages_per_step + ii) * page).astype(F32)
        m_new = jnp.maximum(m_new, jnp.max(s, axis=-1, keepdims=True) + off)
        s_parts.append((s, off))
    alpha = jnp.exp(m_old - m_new)
    l_new = alpha * l_ref[...]
    acc = alpha * acc_ref[...]
    for ii, (s, off) in enumerate(s_parts):
        p = jnp.exp(s - (m_new - off))
        l_new = l_new + jnp.sum(p, axis=-1, keepdims=True)
        acc = acc + jnp.dot(p.astype(BF16), v_refs[ii][0, 0].astype(BF16), preferred_element_type=F32)
    m_ref[...] = m_new
    l_ref[...] = l_new
    acc_ref[...] = acc

    @pl.when(step == pl.num_programs(1) - 1)
    def _():
        lam = _diff_lambda(lam_ref, lam_init)
        nw = nw_ref[...]
        a = acc_ref[...] / l_ref[...]
        for h in range(n_heads):
            r0 = h * 2 * ts
            o_h = a[r0:r0 + ts] - lam * a[r0 + ts:r0 + 2 * ts]
            o_ref[0, :, h * LANES:(h + 1) * LANES] = _rms(o_h, nw) * (1.0 - lam_init)


def _attn_sample(proj, k_all, v_all, cache_k, cache_v, page_table, lam_p, subln_w, *, layer, n_heads, col0, lam_init):
    bsz, ts, _ = proj.shape
    d = n_heads * LANES
    pcols = cache_k.shape[2]
    page = pcols // n_heads
    n_pages = page_table.shape[1]
    pps = _pick_tile(n_pages, (PAGES_PER_STEP, 4, 2, 1))
    assert ts & (ts - 1) == 0 and n_heads & (n_heads - 1) == 0 and cache_k.shape[3] == LANES
    n_rows = n_heads * 2 * ts

    new_spec = pl.BlockSpec((None, 1, ts, d), lambda b, s, pt: (layer, b, 0, 0))

    def page_spec(ii):
        return pl.BlockSpec((1, 1, pcols, LANES), lambda b, s, pt, ii=ii: (layer, pt[b, s * pps + ii], 0, 0))

    grid_spec = pltpu.PrefetchScalarGridSpec(
        num_scalar_prefetch=1,
        grid=(bsz, n_pages // pps),
        in_specs=[pl.BlockSpec((1, ts, d), lambda b, s, pt: (b, 0, col0)), new_spec, new_spec]
        + [page_spec(ii) for ii in range(pps)] * 2
        + [pl.BlockSpec(lam_p.shape, lambda b, s, pt: (0, 0)), pl.BlockSpec((1, LANES), lambda b, s, pt: (0, 0))],
        out_specs=pl.BlockSpec((1, ts, d), lambda b, s, pt: (b, 0, 0)),
        scratch_shapes=[pltpu.VMEM((n_rows, LANES), BF16), pltpu.VMEM((n_rows, pcols), F32),
                        pltpu.VMEM((n_rows, 1), F32), pltpu.VMEM((n_rows, 1), F32), pltpu.VMEM((n_rows, LANES), F32)])
    return pl.pallas_call(
        functools.partial(_attn_sample_kernel, n_heads=n_heads, ts=ts, pages_per_step=pps, page=page,
                          past=n_pages * page, lam_init=lam_init),
        out_shape=jax.ShapeDtypeStruct((bsz, ts, d), F32),
        grid_spec=grid_spec,
        compiler_params=_params("parallel", "arbitrary"),
        name="attn_sample",
    )(page_table, proj, k_all, v_all, *([cache_k] * pps), *([cache_v] * pps), lam_p, subln_w.reshape(1, LANES))


def _ssd_kernel(*refs, hpg, headdim, conv_k, has_state):
    (x_ref, b_ref, c_ref, z_ref, dt_ref, wx_ref, wb_ref, wc_ref, bx_ref, bb_ref, bc_ref,
     dtb_ref, alog_ref, dvec_ref, nw_ref) = refs[:15]
    if has_state:
        cx_ref, cb_ref, cc_ref, s0_ref = refs[15:19]
        refs = refs[19:]
    else:
        refs = refs[15:]
    o_ref, so_ref, st_ref, buf_ref, cumt_ref, dtt_ref = refs
    g = pl.program_id(1)
    t = pl.program_id(2)
    tb = x_ref.shape[1]
    c = SSD_CHUNK
    gps = st_ref.shape[0]
    xw = x_ref.shape[2]
    bw = b_ref.shape[2]
    gw = xw // gps
    n_state = bw // gps
    assert headdim & (headdim - 1) == 0 and n_state == LANES and tb <= c
    pad = SUBLANES

    @pl.when(t == 0)
    def _():
        if has_state:
            st_ref[...] = s0_ref[0]
            buf_ref[0:pad, 0:xw] = cx_ref[0]
            buf_ref[0:pad, xw:xw + bw] = cb_ref[0]
            buf_ref[0:pad, xw + bw:] = cc_ref[0]
        else:
            st_ref[...] = jnp.zeros_like(st_ref)
            buf_ref[0:pad, :] = jnp.zeros((pad, buf_ref.shape[1]), F32)

    buf_ref[pad:pad + c, 0:xw] = _pad_rows(x_ref[0], c)
    buf_ref[pad:pad + c, xw:xw + bw] = _pad_rows(b_ref[0], c)
    buf_ref[pad:pad + c, xw + bw:] = _pad_rows(c_ref[0], c)
    w_all = jnp.concatenate([wx_ref[...], wb_ref[...], wc_ref[...]], axis=1)
    conv = jnp.concatenate([bx_ref[...], bb_ref[...], bc_ref[...]], axis=1)
    for j in range(conv_k):
        r0 = pad - (conv_k - 1) + j
        conv = conv + w_all[j:j + 1] * buf_ref[r0:r0 + c, :]
    tail = buf_ref[c:c + pad, :]
    buf_ref[0:pad, :] = tail
    act = _silu(conv)

    x_dt = dt_ref[0] + dtb_ref[...]
    dt = jnp.maximum(x_dt, 0.0) + jnp.log1p(jnp.exp(-jnp.abs(x_dt)))
    dt = _pad_rows(dt, c)
    if tb < c:
        dt = jnp.where(lax.broadcasted_iota(jnp.int32, dt.shape, 0) < tb, dt, 0.0)
    a = -jnp.exp(alog_ref[...])
    cum = _cumsum_rows(dt * a)
    cumt_ref[...] = cum.T
    dtt_ref[...] = dt.T
    head0 = g * (gps * hpg)
    shift = lax.rem(LANES - head0, LANES)
    cum_g = pltpu.roll(cum, shift, 1)
    dt_g = pltpu.roll(dt, shift, 1)

    tri = _tril(c)
    log_p = int(math.log2(headdim))
    lane_head = lax.shift_right_logical(lax.broadcasted_iota(jnp.int32, (c, gw), 1), log_p)
    row_head = lax.shift_right_logical(lax.broadcasted_iota(jnp.int32, (gw, n_state), 0), log_p)
    dvec = dvec_ref[...]
    nw = nw_ref[...]
    z = z_ref[0]
    for gi in range(gps):
        xs = act[:, gi * gw:(gi + 1) * gw]
        b_c = act[:, xw + gi * n_state:xw + (gi + 1) * n_state].astype(BF16)
        c_c = act[:, xw + bw + gi * n_state:xw + bw + (gi + 1) * n_state].astype(BF16)
        cb = lax.dot_general(c_c, b_c, _TRANS_B, preferred_element_type=F32)
        ms = []
        ecum = jnp.zeros((c, gw), F32)
        wsel = jnp.zeros((c, gw), F32)
        dec = jnp.zeros((gw, n_state), F32)
        for r in range(hpg):
            hl = gi * hpg + r
            cc = cum_g[:, hl:hl + 1]
            crow = cumt_ref[pl.ds(head0 + hl, 1), :]
            drow = dtt_ref[pl.ds(head0 + hl, 1), :]
            ms.append(jnp.where(tri, cb * jnp.exp(cc - crow) * drow, 0.0).astype(BF16))
            last = cum_g[c - 1:c, hl:hl + 1]
            ecum = jnp.where(lane_head == r, jnp.exp(cc), ecum)
            wsel = jnp.where(lane_head == r, dt_g[:, hl:hl + 1] * jnp.exp(last - cc), wsel)
            dec = jnp.where(row_head == r, jnp.exp(last), dec)
        y_all = jnp.dot(jnp.concatenate(ms, axis=0), xs.astype(BF16), preferred_element_type=F32)
        y = jnp.zeros((c, gw), F32)
        for r in range(hpg):
            y = jnp.where(lane_head == r, y_all[r * c:(r + 1) * c], y)
        sp = st_ref[gi]
        y = y + lax.dot_general(c_c, sp.astype(BF16), _TRANS_B, preferred_element_type=F32) * ecum
        upd = lax.dot_general((xs * wsel).astype(BF16), b_c, _TRANS_A, preferred_element_type=F32)
        st_ref[gi] = dec * sp + upd
        cols = slice(gi * gw, (gi + 1) * gw)
        y = y + dvec[:, cols] * xs
        y = y[:tb] * _silu(z[:, cols])
        o_ref[0, :, cols] = _rms(y, nw[:, cols])

    @pl.when(t == pl.num_programs(2) - 1)
    def _():
        so_ref[0] = st_ref[...]


def _ssd(proj, dtp, conv_w, conv_b, dt_bias, a_log, d_vec, norm_w, conv_state, s0, *, z_col, xbc_col, groups, hpg,
         headdim, n_state):
    bsz, t, _ = proj.shape
    gw = hpg * headdim
    d_inner = groups * gw
    conv_k = conv_w.shape[0]
    tb = SSD_CHUNK if t >= SSD_CHUNK else t
    gps = SSD_GROUPS_PER_STEP if groups % SSD_GROUPS_PER_STEP == 0 else 1
    xw, bw = gps * gw, gps * n_state
    assert t % tb == 0 and gw % LANES == 0 and n_state == LANES
    assert z_col % xw == 0 and xbc_col % xw == 0 and (xbc_col + d_inner) % bw == 0 and d_inner % bw == 0
    xb, bb = xbc_col // xw, (xbc_col + d_inner) // bw
    cb_ = bb + groups // gps
    zb = z_col // xw
    wb0 = d_inner // bw
    wc0 = wb0 + groups // gps

    def tok(width, blk0):
        return pl.BlockSpec((1, tb, width), lambda b, g, ti: (b, ti, blk0 + g))

    def par(rows, width, blk0):
        return pl.BlockSpec((rows, width), lambda b, g, ti: (0, blk0 + g))

    def full(shape):
        return pl.BlockSpec(shape, lambda b, g, ti: (0,) * len(shape))

    in_specs = [tok(xw, xb), tok(bw, bb), tok(bw, cb_), tok(xw, zb),
                pl.BlockSpec((1, tb, LANES), lambda b, g, ti: (b, ti, 0)),
                par(conv_k, xw, 0), par(conv_k, bw, wb0), par(conv_k, bw, wc0),
                par(1, xw, 0), par(1, bw, wb0), par(1, bw, wc0),
                full((1, LANES)), full((1, LANES)), par(1, xw, 0), par(1, xw, 0)]
    args = [proj, proj, proj, proj, dtp, conv_w, conv_w, conv_w, conv_b, conv_b, conv_b,
            dt_bias, a_log, d_vec, norm_w]
    if s0 is not None:
        def st(width, blk0):
            return pl.BlockSpec((1, SUBLANES, width), lambda b, g, ti: (b, 0, blk0 + g))
        in_specs += [st(xw, 0), st(bw, wb0), st(bw, wc0),
                     pl.BlockSpec((1, gps, gw, n_state), lambda b, g, ti: (b, g, 0, 0))]
        args += [conv_state, conv_state, conv_state, s0]
    return pl.pallas_call(
        functools.partial(_ssd_kernel, hpg=hpg, headdim=headdim, conv_k=conv_k, has_state=s0 is not None),
        out_shape=(jax.ShapeDtypeStruct((bsz, t, d_inner), F32),
                   jax.ShapeDtypeStruct((bsz, groups, gw, n_state), F32)),
        grid=(bsz, groups // gps, t // tb),
        in_specs=in_specs,
        out_specs=(pl.BlockSpec((1, tb, xw), lambda b, g, ti: (b, ti, g)),
                   pl.BlockSpec((1, gps, gw, n_state), lambda b, g, ti: (b, g, 0, 0))),
        scratch_shapes=[pltpu.VMEM((gps, gw, n_state), F32),
                        pltpu.VMEM((SSD_CHUNK + SUBLANES, xw + 2 * bw), F32),
                        pltpu.VMEM((LANES, SSD_CHUNK), F32), pltpu.VMEM((LANES, SSD_CHUNK), F32)],
        compiler_params=_params("parallel", "parallel", "arbitrary"),
        name="ssd",
    )(*args)


def _merge_kernel(x_ref, oa_ref, ob_ref, yc_ref, g0_ref, g1_ref, g2_ref, wa_ref, wb_ref, wc_ref, wo_ref, o_ref):
    def branch(gate_ref, v_ref, w_ref):
        return jax.nn.sigmoid(gate_ref[...]) * jnp.dot(v_ref[...].astype(BF16), w_ref[...], preferred_element_type=F32)

    merged = branch(g0_ref, oa_ref, wa_ref) + branch(g1_ref, ob_ref, wb_ref) + branch(g2_ref, yc_ref, wc_ref)
    o_ref[...] = x_ref[...] + jnp.dot(merged.astype(BF16), wo_ref[...], preferred_element_type=F32)


def _merge(x, o_a, o_b, y_c, proj, w_a, w_b, w_c, w_o, *, gate_col):
    n, d = x.shape
    tm = _pick_tile(n, (256, 128, 64, 32, 16, 8))
    gb = gate_col // d

    def tok(width, blk=0):
        return pl.BlockSpec((tm, width), lambda i, blk=blk: (i, blk))

    def weight(w):
        return pl.BlockSpec(w.shape, lambda i: (0, 0), pipeline_mode=pl.Buffered(1))

    return pl.pallas_call(
        _merge_kernel,
        out_shape=jax.ShapeDtypeStruct((n, d), F32),
        grid=(n // tm,),
        in_specs=[tok(d), tok(d), tok(d), tok(y_c.shape[1]), tok(d, gb), tok(d, gb + 1), tok(d, gb + 2),
                  weight(w_a), weight(w_b), weight(w_c), weight(w_o)],
        out_specs=tok(d),
        compiler_params=_params("parallel"),
        name="merge",
    )(x, o_a, o_b, y_c, proj, proj, proj, w_a, w_b, w_c, w_o)


def _xattn_kernel(x_ref, mk_ref, mv_ref, g_ref, wq_ref, wo_ref, o_ref, *, n_heads):
    x = x_ref[0]
    d = x.shape[1]
    dh = d // n_heads
    q = jnp.dot(_rms(x, g_ref[...]).astype(BF16), wq_ref[...], preferred_element_type=F32) * (dh ** -0.5)
    outs = []
    for h in range(n_heads):
        cols = slice(h * dh, (h + 1) * dh)
        k_b = mk_ref[0, :, cols].astype(BF16)
        v_b = mv_ref[0, :, cols].astype(BF16)
        s = lax.dot_general(q[:, cols].astype(BF16), k_b, _TRANS_B, preferred_element_type=F32)
        p = jnp.exp(s - jnp.max(s, axis=-1, keepdims=True))
        o_h = jnp.dot(p.astype(BF16), v_b, preferred_element_type=F32)
        outs.append(o_h / jnp.sum(p, axis=-1, keepdims=True))
    o_m = jnp.concatenate(outs, axis=1).astype(BF16)
    o_ref[0] = x + jnp.dot(o_m, wo_ref[...], preferred_element_type=F32)


def _xattn(x, mem_k, mem_v, g, w_q, w_o, *, n_heads):
    bsz, t, d = x.shape
    m = mem_k.shape[1]
    tq = _pick_tile(t, (512, 256, 128, 64, 32, 16, 8))

    def weight(w):
        return pl.BlockSpec(w.shape, lambda b, ti: (0, 0), pipeline_mode=pl.Buffered(1))

    return pl.pallas_call(
        functools.partial(_xattn_kernel, n_heads=n_heads),
        out_shape=jax.ShapeDtypeStruct((bsz, t, d), F32),
        grid=(bsz, t // tq),
        in_specs=[pl.BlockSpec((1, tq, d), lambda b, ti: (b, ti, 0)),
                  pl.BlockSpec((1, m, d), lambda b, ti: (b, 0, 0)),
                  pl.BlockSpec((1, m, d), lambda b, ti: (b, 0, 0)),
                  pl.BlockSpec((1, d), lambda b, ti: (0, 0)),
                  weight(w_q), weight(w_o)],
        out_specs=pl.BlockSpec((1, tq, d), lambda b, ti: (b, ti, 0)),
        compiler_params=_params("parallel", "parallel"),
        name="xattn",
    )(x, mem_k, mem_v, g.reshape(1, d), w_q, w_o)


def _ffn_kernel(*refs, conv_k, has_state, final_norm):
    x_ref, g_ref, wg_ref, wv_ref, wd_ref, cw_ref, cb_ref = refs[:7]
    refs = refs[7:]
    if has_state:
        cs_ref, refs = refs[0], refs[1:]
    if final_norm:
        fn_ref, refs = refs[0], refs[1:]
    o_ref, fo_ref, hn_ref, acc_ref, buf_ref, tail_ref = refs
    t = pl.program_id(1)
    f = pl.program_id(2)
    tt = x_ref.shape[1]
    pad = SUBLANES

    @pl.when(f == 0)
    def _():
        hn_ref[...] = _rms(x_ref[0], g_ref[...]).astype(BF16)
        acc_ref[...] = jnp.zeros_like(acc_ref)

    @pl.when(t == 0)
    def _():
        if has_state:
            tail_ref[f] = cs_ref[0]
        else:
            tail_ref[f] = jnp.zeros(tail_ref.shape[1:], F32)

    hn = hn_ref[...]
    gate = jnp.dot(hn, wg_ref[...], preferred_element_type=F32)
    val = jnp.dot(hn, wv_ref[...], preferred_element_type=F32)
    buf_ref[0:pad, :] = tail_ref[f]
    buf_ref[pad:pad + tt, :] = gate
    cw = cw_ref[...]
    conv = cb_ref[...]
    for j in range(conv_k):
        r0 = pad - (conv_k - 1) + j
        conv = conv + cw[j:j + 1] * buf_ref[r0:r0 + tt, :]
    tail_ref[f] = buf_ref[tt:tt + pad, :]
    fo_ref[0, 0] = buf_ref[pad + tt - (conv_k - 1):pad + tt, :]
    acc_ref[...] += jnp.dot((_silu(conv) * val).astype(BF16), wd_ref[...], preferred_element_type=F32)

    @pl.when(f == pl.num_programs(2) - 1)
    def _():
        y = x_ref[0] + acc_ref[...]
        if final_norm:
            y = _rms(y, fn_ref[...])
        o_ref[0] = y


def _ffn(x, g, w_up, w_down, conv_w, conv_b, conv_state, final_w):
    bsz, t, d = x.shape
    ff = w_down.shape[0]
    conv_k = conv_w.shape[0]
    tt = _pick_tile(t, (512, 256, 128, 64, 32, 16, 8))
    tf = _pick_tile(ff, (1408, 1024, 512, 256, 128))
    nf = ff // tf
    assert tt >= conv_k - 1
    in_specs = [pl.BlockSpec((1, tt, d), lambda b, ti, f: (b, ti, 0)),
                pl.BlockSpec((1, d), lambda b, ti, f: (0, 0)),
                pl.BlockSpec((d, tf), lambda b, ti, f: (0, f)),
                pl.BlockSpec((d, tf), lambda b, ti, f: (0, nf + f)),
                pl.BlockSpec((tf, d), lambda b, ti, f: (f, 0)),
                pl.BlockSpec((conv_k, tf), lambda b, ti, f: (0, f)),
                pl.BlockSpec((1, tf), lambda b, ti, f: (0, f))]
    args = [x, g.reshape(1, d), w_up, w_up, w_down, conv_w, conv_b.reshape(1, ff)]
    if conv_state is not None:
        in_specs.append(pl.BlockSpec((1, SUBLANES, tf), lambda b, ti, f: (b, 0, f)))
        args.append(conv_state)
    if final_w is not None:
        in_specs.append(pl.BlockSpec((1, d), lambda b, ti, f: (0, 0)))
        args.append(final_w.reshape(1, d))
    y, tails = pl.pallas_call(
        functools.partial(_ffn_kernel, conv_k=conv_k, has_state=conv_state is not None,
                          final_norm=final_w is not None),
        out_shape=(jax.ShapeDtypeStruct((bsz, t, d), F32),
                   jax.ShapeDtypeStruct((bsz, t // tt, conv_k - 1, ff), F32)),
        grid=(bsz, t // tt, nf),
        in_specs=in_specs,
        out_specs=(pl.BlockSpec((1, tt, d), lambda b, ti, f: (b, ti, 0)),
                   pl.BlockSpec((1, 1, conv_k - 1, tf), lambda b, ti, f: (b, ti, 0, f))),
        scratch_shapes=[pltpu.VMEM((tt, d), BF16), pltpu.VMEM((tt, d), F32),
                        pltpu.VMEM((tt + SUBLANES, tf), F32), pltpu.VMEM((nf, SUBLANES, tf), F32)],
        compiler_params=_params("parallel", "arbitrary", "arbitrary"),
        name="ffn",
    )(*args)
    return y, tails[:, -1]


def _pad_state_rows(s):
    return jnp.pad(s, ((0, 0), (SUBLANES - s.shape[1], 0), (0, 0)))


def _pad_lanes(v):
    return jnp.pad(v, (0, LANES - v.shape[0])).reshape(1, LANES)


def kernel(x_prompt, x_sample, cache_k, cache_v, cache_mem_k, cache_mem_v, state_hgrn, state_ssm, state_conv, state_ffn_conv, page_table, mem_prompt, norm_mix, w_in, hgrn_lb_logits, hgrn_out_norm, diff_lambda, diff_subln, ssm_conv_w, ssm_conv_b, ssm_dt_bias, ssm_a_log, ssm_d, ssm_norm, w_br_a, w_br_b, w_br_c, w_out, norm_mem_q, norm_mem_kv, w_mq, w_mk, w_mv, w_mo, norm_ffn, w_up, ffn_conv_w, ffn_conv_b, w_down, norm_final):
    depth = w_in.shape[0]
    bp, tp, d = x_prompt.shape
    bs, ts, _ = x_sample.shape
    n_heads = d // LANES
    groups, hpg, headdim, n_state = state_ssm.shape[2:]
    d_inner = groups * hpg * headdim
    conv_dim = state_conv.shape[-1]
    ssm_heads = groups * hpg
    mem_len, mem_heads = cache_mem_k.shape[2], cache_mem_k.shape[3]
    n_phys, page = cache_k.shape[1], cache_k.shape[2]
    assert ssm_heads <= LANES and conv_dim == d_inner + 2 * groups * n_state
    kv_col = 5 * d
    dt_col = 7 * d + d_inner + conv_dim
    z_col = kv_col
    xbc_col = z_col + d_inner
    gate_col = xbc_col + conv_dim

    slopes = jnp.exp2(-8.0 * jnp.arange(1, n_heads + 1, dtype=F32) / n_heads)
    cache_k2 = cache_k.reshape(depth, n_phys, page * n_heads, LANES)
    cache_v2 = cache_v.reshape(depth, n_phys, page * n_heads, LANES)

    xp, xs = x_prompt, x_sample
    kv_p, kv_s = (None, None), (None, None)
    outs = [[] for _ in range(10)]
    for l in range(depth):
        lam_init = 0.8 - 0.6 * math.exp(-0.3 * l)
        last = l == depth - 1
        w_main = jnp.concatenate([w_in[l][:, :kv_col], w_in[l][:, kv_col + 2 * d:dt_col],
                                  w_in[l][:, dt_col + ssm_heads:]], axis=1).astype(BF16)
        w_kb = w_in[l][:, kv_col:kv_col + d].astype(BF16)
        w_vb = w_in[l][:, kv_col + d:kv_col + 2 * d].astype(BF16)
        w_dt = jnp.pad(w_in[l][:, dt_col:dt_col + ssm_heads], ((0, 0), (0, LANES - ssm_heads))).astype(BF16)
        wa, wb, wc, wo = (w[l].astype(BF16) for w in (w_br_a, w_br_b, w_br_c, w_out))
        wq, wk, wv, wmo = (w[l].astype(BF16) for w in (w_mq, w_mk, w_mv, w_mo))
        wup, wdn = w_up[l].astype(BF16), w_down[l].astype(BF16)
        dt_bias, a_log = _pad_lanes(ssm_dt_bias[l]), _pad_lanes(ssm_a_log[l])
        d_vec = jnp.repeat(ssm_d[l], headdim).reshape(1, d_inner)
        ssm_nw = ssm_norm[l].reshape(1, d_inner)
        conv_b = ssm_conv_b[l].reshape(1, conv_dim)

        mem_n = mem_prompt.reshape(bp * mem_len, d)
        mk = _norm_matmul(mem_n, norm_mem_kv[l], wk).reshape(bp, mem_len, d)
        mv = _norm_matmul(mem_n, norm_mem_kv[l], wv).reshape(bp, mem_len, d)

        def trunk(x, kv, mem_k, mem_v, hgrn_s0, ssm_s0, conv_s, ffn_s, attend):
            bsz, t, _ = x.shape
            x2 = x.reshape(bsz * t, d)
            proj2 = _norm_matmul(x2, norm_mix[l], w_main)
            proj = proj2.reshape(bsz, t, -1)
            dtp = _norm_matmul(x2, norm_mix[l], w_dt).reshape(bsz, t, LANES)
            kv = (_norm_matmul_into(x2, norm_mix[l], w_kb, kv[0], l, depth),
                  _norm_matmul_into(x2, norm_mix[l], w_vb, kv[1], l, depth))
            o_a, s_a = _hgrn(proj, hgrn_lb_logits, hgrn_out_norm[l], hgrn_s0, layer=l, n_heads=n_heads, col0=0)
            o_b = attend(proj, kv[0].reshape(depth, bsz, t, d), kv[1].reshape(depth, bsz, t, d))
            y_c, s_c = _ssd(proj, dtp, ssm_conv_w[l], conv_b, dt_bias, a_log, d_vec, ssm_nw, conv_s,
                            None if ssm_s0 is None else ssm_s0.reshape(bsz, groups, hpg * headdim, n_state),
                            z_col=z_col, xbc_col=xbc_col, groups=groups, hpg=hpg, headdim=headdim, n_state=n_state)
            x2 = _merge(x2, o_a.reshape(bsz * t, d), o_b.reshape(bsz * t, d), y_c.reshape(bsz * t, d_inner), proj2,
                        wa, wb, wc, wo, gate_col=gate_col)
            x3 = _xattn(x2.reshape(bsz, t, d), mem_k, mem_v, norm_mem_q[l], wq, wmo, n_heads=mem_heads)
            x3, ffn_new = _ffn(x3, norm_ffn[l], wup, wdn, ffn_conv_w[l], ffn_conv_b[l], ffn_s,
                               norm_final if last else None)
            conv_new = proj[:, t - (ssm_conv_w.shape[1] - 1):, xbc_col:xbc_col + conv_dim]
            return x3, kv, (s_a, s_c.reshape(bsz, groups, hpg, headdim, n_state), conv_new, ffn_new)

        xp, kv_p, st_p = trunk(
            xp, kv_p, mk, mv, None, None, None, None,
            lambda proj, k_all, v_all: _attn_prompt(proj, k_all, v_all, slopes, diff_lambda[l], diff_subln[l], layer=l,
                                                    n_heads=n_heads, col0=4 * n_heads, lam_init=lam_init))
        xs, kv_s, st_s = trunk(
            xs, kv_s, cache_mem_k[l].reshape(bs, mem_len, d), cache_mem_v[l].reshape(bs, mem_len, d),
            state_hgrn[l], state_ssm[l], _pad_state_rows(state_conv[l]), _pad_state_rows(state_ffn_conv[l]),
            lambda proj, k_all, v_all: _attn_sample(proj, k_all, v_all, cache_k2, cache_v2, page_table, diff_lambda[l],
                                                    diff_subln[l], layer=l, n_heads=n_heads, col0=4, lam_init=lam_init))
        new = (mk.reshape(bp, mem_len, mem_heads, d // mem_heads), mv.reshape(bp, mem_len, mem_heads, d // mem_heads),
               st_p[0], st_s[0], st_p[1], st_s[1], st_p[2], st_s[2], st_p[3], st_s[3])
        for acc, val in zip(outs, new):
            acc.append(val)
    kv_out = tuple(a.reshape(depth, b, t, n_heads, LANES)
                   for a, b, t in ((kv_p[0], bp, tp), (kv_p[1], bp, tp), (kv_s[0], bs, ts), (kv_s[1], bs, ts)))
    return (xp, xs) + kv_out + tuple(jnp.stack(o) for o in outs)
```
